```python
import math
import jax, jax.numpy as jnp
from jax import lax
import numpy as np

D_MODEL = 1024
BATCH = 32
SEQ = 2048
DEPTH = 4

D_MIX = 2 * D_MODEL
M_WIDTH = 3 * D_MIX // 8
M_HEADS = 4
M_HEAD_DIM = M_WIDTH // M_HEADS
M_QKV_BLOCK = 4
M_CONV = 5
M_CHUNK = 64
H_WIDTH = 3 * D_MIX // 8
H_EXPAND = 128
H_HEADS = H_WIDTH // H_EXPAND
H_CHUNK = 16
A_WIDTH = D_MIX - M_WIDTH - H_WIDTH
A_HEADS = 4
A_VDIM = A_WIDTH // A_HEADS
A_QKDIM = A_VDIM // 2
ROPE_DIM = A_QKDIM // 4
ROPE_THETA = 500000.0
Q_BLOCK = 128
IN_COLS = 3 * M_WIDTH + 5 * H_WIDTH + 4 * A_WIDTH
EPS = 1e-6

kernel_name = "hybrid_mlstm_hgrn2_diffattn_encoder"


def _split_points():
    sizes = [M_WIDTH] * 3 + [H_WIDTH] * 5 + [A_WIDTH] * 4
    return [int(c) for c in np.cumsum(sizes)[:-1]]


def rms_norm(x, g):
    xf = x.astype(jnp.float32)
    y = xf * lax.rsqrt(jnp.mean(xf * xf, axis=-1, keepdims=True) + EPS)
    return (y * g.astype(jnp.float32)).astype(x.dtype)


def head_rms_norm(x, g, n_heads):
    B, S, W = x.shape
    xf = x.astype(jnp.float32).reshape(B, S, n_heads, W // n_heads)
    y = xf * lax.rsqrt(jnp.mean(xf * xf, axis=-1, keepdims=True) + EPS)
    return (y.reshape(B, S, W) * g.astype(jnp.float32)).astype(x.dtype)


def centred_conv(x, w, b):
    K = w.shape[0]
    pad = K // 2
    y = lax.conv_general_dilated(x, w[:, None, :], window_strides=(1,), padding=[(pad, pad)],
                                 dimension_numbers=("NWC", "WIO", "NWC"),
                                 feature_group_count=x.shape[-1])
    return y + b


def block_diag_proj(x, w):
    B, S, W = x.shape
    nb, blk, _ = w.shape
    return jnp.einsum("bsgi,gio->bsgo", x.reshape(B, S, nb, blk), w).reshape(B, S, W)


def to_heads(t, n):
    B, S, W = t.shape
    return t.reshape(B, S, n, W // n).transpose(0, 2, 1, 3)


def from_heads(t):
    B, H, S, D = t.shape
    return t.transpose(0, 2, 1, 3).reshape(B, S, H * D)


def _chunk(t, L):
    B, H, S = t.shape[:3]
    return jnp.moveaxis(t.reshape((B, H, S // L, L) + t.shape[3:]), 2, 0)


def _unchunk(t):
    t = jnp.moveaxis(t, 0, 2)
    return t.reshape(t.shape[:2] + (t.shape[2] * t.shape[3],) + t.shape[4:])


def _flip_seq(t):
    return jnp.flip(t, axis=2)


def mlstm_chunkwise(q, k, v, i_pre, f_pre):
    B, H, S, D = q.shape
    L = M_CHUNK
    qf = q.astype(jnp.float32) * (D ** -0.5)
    kf, vf = k.astype(jnp.float32), v.astype(jnp.float32)
    log_f = jax.nn.log_sigmoid(f_pre.astype(jnp.float32))
    log_i = i_pre.astype(jnp.float32)
    xs = (_chunk(qf, L), _chunk(kf, L), _chunk(vf, L), _chunk(log_i, L), _chunk(log_f, L))
    mask = jnp.tril(jnp.ones((L, L), dtype=bool))

    def step(carry, inp):
        C, n, m = carry
        qb, kb, vb, ib, fb = inp
        b = jnp.cumsum(fb, axis=-1)
        bL = b[..., -1]
        d_ts = jnp.where(mask, b[..., :, None] - b[..., None, :] + ib[..., None, :], -jnp.inf)
        inter = b + m[..., None]
        m_t = jnp.maximum(jnp.max(d_ts, axis=-1), inter)
        w = jnp.exp(d_ts - m_t[..., None]) * jnp.einsum("bhtd,bhsd->bhts", qb, kb)
        g_inter = jnp.exp(inter - m_t)
        num = jnp.einsum("bhts,bhsd->bhtd", w, vb) + g_inter[..., None] * jnp.einsum("bhtd,bhde->bhte", qb, C)
        den = jnp.sum(w, axis=-1) + g_inter * jnp.einsum("bhtd,bhd->bht", qb, n)
        h = num / jnp.maximum(jnp.abs(den), jnp.exp(-m_t))[..., None]
        dec_s = bL[..., None] - b + ib
        m_new = jnp.maximum(bL + m, jnp.max(dec_s, axis=-1))
        a = jnp.exp(bL + m - m_new)
        ws = jnp.exp(dec_s - m_new[..., None])
        C_new = a[..., None, None] * C + jnp.einsum("bhs,bhsd,bhse->bhde", ws, kb, vb)
        n_new = a[..., None] * n + jnp.einsum("bhs,bhsd->bhd", ws, kb)
        return (C_new, n_new, m_new), h

    init = (jnp.zeros((B, H, D, D), jnp.float32), jnp.zeros((B, H, D), jnp.float32),
            jnp.zeros((B, H), jnp.float32))
    _, hs = lax.scan(step, init, xs)
    return _unchunk(hs).astype(q.dtype)


def mlstm_bidirectional(q, k, v, i_fwd, f_fwd, i_bwd, f_bwd):
    fwd = mlstm_chunkwise(q, k, v, i_fwd, f_fwd)
    bwd = _flip_seq(mlstm_chunkwise(_flip_seq(q), _flip_seq(k), _flip_seq(v),
                                    _flip_seq(i_bwd), _flip_seq(f_bwd)))
    return fwd + bwd


def hgrn2_chunkwise(q, f, v):
    B, H, S, Dk = q.shape
    Dv = v.shape[-1]
    L = H_CHUNK
    ff = f.astype(jnp.float32)
    xs = (_chunk(q.astype(jnp.float32), L), _chunk(1.0 - ff, L),
          _chunk(v.astype(jnp.float32), L), _chunk(jnp.log(ff), L))
    mask = jnp.tril(jnp.ones((L, L), dtype=bool))[:, :, None]

    def step(S_state, inp):
        qb, kb, vb, gb = inp
        A = jnp.cumsum(gb, axis=-2)
        diff = A[..., :, None, :] - A[..., None, :, :]
        decay = jnp.exp(jnp.where(mask, diff, -jnp.inf))
        scores = jnp.einsum("bhtc,bhtsc,bhsc->bhts", qb, decay, kb)
        o = jnp.einsum("bhts,bhsv->bhtv", scores, vb) + jnp.einsum("bhtc,bhcv->bhtv", qb * jnp.exp(A), S_state)
        AL = A[..., -1, :]
        S_new = jnp.exp(AL)[..., None] * S_state + jnp.einsum("bhsc,bhsv->bhcv", kb * jnp.exp(AL[..., None, :] - A), vb)
        return S_new, o

    _, os_ = lax.scan(step, jnp.zeros((B, H, Dk, Dv), jnp.float32), xs)
    return _unchunk(os_).astype(q.dtype)


def hgrn2_bidirectional(q, f_fwd, f_bwd, v):
    fwd = hgrn2_chunkwise(q, f_fwd, v)
    bwd = _flip_seq(hgrn2_chunkwise(_flip_seq(q), _flip_seq(f_bwd), _flip_seq(v)))
    return fwd + bwd


def hgrn2_lower_bounds(lb_logits):
    p = jax.nn.softmax(lb_logits.astype(jnp.float32), axis=0)
    c = jnp.cumsum(p, axis=0)
    return c - c[:1]


def rope_partial(x, positions):
    half = ROPE_DIM // 2
    inv = ROPE_THETA ** (-jnp.arange(half, dtype=jnp.float32) / half)
    ang = positions.astype(jnp.float32)[..., None] * inv
    cos, sin = jnp.cos(ang)[:, :, None, :], jnp.sin(ang)[:, :, None, :]
    xr = x[..., :ROPE_DIM].astype(jnp.float32)
    x1, x2 = xr[..., :half], xr[..., half:]
    rot = jnp.concatenate([x1 * cos - x2 * sin, x2 * cos + x1 * sin], axis=-1).astype(x.dtype)
    return jnp.concatenate([rot, x[..., ROPE_DIM:]], axis=-1)


def differential_attention(q, k, v, lam, positions):
    B, S, N2, Dq = q.shape
    Hh, Dv = v.shape[2], v.shape[3]
    q = rope_partial(q, positions) * (Dq ** -0.5)
    k = rope_partial(k, positions)
    nb = S // Q_BLOCK
    qb = q.reshape(B, nb, Q_BLOCK, N2, Dq).transpose(1, 0, 2, 3, 4)

    def block(qi):
        s = jnp.einsum("bqnd,bknd->bnqk", qi, k).astype(jnp.float32)
        p = jax.nn.softmax(s, axis=-1).reshape(B, Hh, 2, Q_BLOCK, S)
        w = p[:, :, 0] - lam * p[:, :, 1]
        return jnp.einsum("bhqk,bkhv->bqhv", w.astype(v.dtype), v)

    o = lax.map(block, qb)
    return o.transpose(1, 0, 2, 3, 4).reshape(B, S, Hh, Dv)


def setup_inputs(seed: int = 0) -> dict:
    key = jax.random.key(seed)
    ks = jax.random.split(key, 20)
    f32 = jnp.float32

    def nrm(k, shape, scale):
        return jax.random.normal(k, shape, f32) * scale

    nblk = M_WIDTH // M_QKV_BLOCK
    f_bias = jnp.linspace(3.0, 6.0, M_HEADS, dtype=f32)
    zeros_h = jnp.zeros((M_HEADS,), f32)
    gate_base = jnp.concatenate([zeros_h, f_bias, zeros_h, f_bias])
    return {
        "x": nrm(ks[0], (BATCH, SEQ, D_MODEL), 1.0),
        "positions": jnp.broadcast_to(jnp.arange(SEQ, dtype=jnp.int32), (BATCH, SEQ)),
        "norm_g": 1.0 + nrm(ks[1], (DEPTH, D_MODEL), 0.02),
        "w_in": nrm(ks[2], (DEPTH, D_MODEL, IN_COLS), D_MODEL ** -0.5),
        "m_conv_w": nrm(ks[3], (DEPTH, M_CONV, M_WIDTH), M_CONV ** -0.5),
        "m_conv_b": nrm(ks[4], (DEPTH, M_WIDTH), 0.02),
        "m_wq": nrm(ks[5], (DEPTH, nblk, M_QKV_BLOCK, M_QKV_BLOCK), M_QKV_BLOCK ** -0.5),
        "m_wk": nrm(ks[6], (DEPTH, nblk, M_QKV_BLOCK, M_QKV_BLOCK), M_QKV_BLOCK ** -0.5),
        "m_wv": nrm(ks[7], (DEPTH, nblk, M_QKV_BLOCK, M_QKV_BLOCK), M_QKV_BLOCK ** -0.5),
        "m_w_gates": nrm(ks[8], (DEPTH, 3 * M_WIDTH, 4 * M_HEADS), (3 * M_WIDTH) ** -0.5),
        "m_b_gates": gate_base + nrm(ks[9], (DEPTH, 4 * M_HEADS), 0.1),
        "m_skip": 1.0 + nrm(ks[10], (DEPTH, M_WIDTH), 0.02),
        "m_norm_g": 1.0 + nrm(ks[11], (DEPTH, M_WIDTH), 0.02),
        "h_lb_logits": nrm(ks[12], (DEPTH, 2, H_WIDTH), 0.5),
        "h_norm_g": 1.0 + nrm(ks[13], (DEPTH, H_WIDTH), 0.02),
        "a_lambda": nrm(ks[14], (DEPTH, 4, A_QKDIM), 0.1),
        "a_norm_g": 1.0 + nrm(ks[15], (DEPTH, A_WIDTH), 0.02),
        "w_out": nrm(ks[16], (DEPTH, D_MIX, D_MODEL), D_MIX ** -0.5),
        "final_g": 1.0 + nrm(ks[17], (D_MODEL,), 0.02),
    }


def reference(x, positions, norm_g, w_in, m_conv_w, m_conv_b, m_wq, m_wk, m_wv, m_w_gates,
              m_b_gates, m_skip, m_norm_g, h_lb_logits, h_norm_g, a_lambda, a_norm_g, w_out,
              final_g):
    split_pts = _split_points()
    lower_bounds = hgrn2_lower_bounds(h_lb_logits)
    for l in range(DEPTH):
        h = rms_norm(x, norm_g[l])
        proj = jnp.einsum("bsd,de->bse", h, w_in[l])
        (xm, om, zm, hq, hff, hfb, hi, hz, aq, ak, av, az) = jnp.split(proj, split_pts, axis=-1)

        xc = jax.nn.silu(centred_conv(xm, m_conv_w[l], m_conv_b[l]))
        mq = block_diag_proj(xc, m_wq[l])
        mk = block_diag_proj(xc, m_wk[l])
        mv = block_diag_proj(xm, m_wv[l])
        gates = jnp.einsum("bsc,cg->bsg", jnp.concatenate([mq, mk, mv], axis=-1), m_w_gates[l]) + m_b_gates[l]
        gates = gates.transpose(0, 2, 1)
        i_f, f_f, i_b, f_b = jnp.split(gates, 4, axis=1)
        hm = mlstm_bidirectional(to_heads(mq, M_HEADS), to_heads(mk, M_HEADS), to_heads(mv, M_HEADS),
                                 i_f, f_f, i_b, f_b)
        hm = jax.nn.sigmoid(om) * from_heads(hm)
        hm = head_rms_norm(hm, m_norm_g[l], M_HEADS) + m_skip[l] * xc
        y_m = hm * jax.nn.silu(zm)

        lb = lower_bounds[l]
        f_fwd = lb[0] + (1.0 - lb[0]) * jax.nn.sigmoid(hff.astype(jnp.float32))
        f_bwd = lb[1] + (1.0 - lb[1]) * jax.nn.sigmoid(hfb.astype(jnp.float32))
        ho = hgrn2_bidirectional(to_heads(hq, H_HEADS), to_heads(f_fwd, H_HEADS),
                                 to_heads(f_bwd, H_HEADS), to_heads(hi, H_HEADS))
        y_h = head_rms_norm(from_heads(ho), h_norm_g[l], H_HEADS) * jax.nn.silu(hz)

        lam_init = 0.8 - 0.6 * math.exp(-0.3 * l)
        lp = a_lambda[l].astype(jnp.float32)
        lam = jnp.exp(jnp.sum(lp[0] * lp[1])) - jnp.exp(jnp.sum(lp[2] * lp[3])) + lam_init
        B_, S_ = x.shape[0], x.shape[1]
        ao = differential_attention(aq.reshape(B_, S_, 2 * A_HEADS, A_QKDIM),
                                    ak.reshape(B_, S_, 2 * A_HEADS, A_QKDIM),
                                    av.reshape(B_, S_, A_HEADS, A_VDIM), lam, positions)
        ao = head_rms_norm(ao.reshape(B_, S_, A_WIDTH), a_norm_g[l], A_HEADS) * (1.0 - lam_init)
        y_a = ao * jax.nn.silu(az)

        y = jnp.concatenate([y_m, y_h, y_a], axis=-1)
        x = x + jnp.einsum("bse,ed->bsd", y, w_out[l])
    return rms_norm(x, final_g)
```

```python
import functools
import math

import numpy as np
import jax
import jax.numpy as jnp
from jax import lax
from jax.experimental import pallas as pl
from jax.experimental.pallas import tpu as pltpu

F32 = jnp.float32
BF16 = jnp.bfloat16

D_MODEL = 1024
DEPTH = 4
M_WIDTH = 768
M_HEADS = 4
M_HEAD_DIM = 192
M_QKV_BLOCK = 4
M_CONV = 5
H_WIDTH = 768
H_HEAD_DIM = 128
H_HEADS = 6
A_WIDTH = 512
A_HEADS = 4
A_VDIM = 128
A_QKDIM = 64
ROPE_DIM = 16
ROPE_THETA = 500000.0
IN_COLS = 3 * M_WIDTH + 5 * H_WIDTH + 4 * A_WIDTH
EPS = 1e-6

LANES = 128
MXU_DIM = 256
VMEM_LIMIT_BYTES = 56 * 1024 * 1024

M_WIN = MXU_DIM
M_WIN_START = (0, 128, 384, 512)
M_WIN_OFF = (0, 64, 0, 64)
M_CHUNK = 128
H_CHUNK = 128
H_LEVELS = 7
Q_BLOCK = 256


def _nt(a, b):
    return lax.dot_general(a, b, (((1,), (1,)), ((), ())), preferred_element_type=F32)


def _tn(a, b):
    return lax.dot_general(a, b, (((0,), (0,)), ((), ())), preferred_element_type=F32)


def _dot(a, b):
    return jnp.dot(a, b, preferred_element_type=F32)


def _sigmoid(x):
    return 1.0 / (1.0 + jnp.exp(-x))


def _silu(x):
    return x * _sigmoid(x)


def _split_bf16(x):
    hi = x.astype(BF16)
    lo = (x - hi.astype(F32)).astype(BF16)
    return jnp.concatenate([hi, lo], axis=1)


def _sum_halves(r):
    return r[:, :LANES] + r[:, LANES:]


def _inproj_body(x_ref, g_ref, w_ref, o_ref, h_ref):
    @pl.when(pl.program_id(1) == 0)
    def _():
        xf = x_ref[...]
        ms = jnp.mean(xf * xf, axis=-1, keepdims=True)
        h_ref[...] = ((xf * lax.rsqrt(ms + EPS)) * g_ref[...]).astype(BF16)

    o_ref[...] = _dot(h_ref[...], w_ref[...]).astype(o_ref.dtype)


def _inproj(x2d, g, w):
    m = x2d.shape[0]
    bm = min(1024, m)
    bn = 2048
    return pl.pallas_call(
        _inproj_body,
        grid=(m // bm, IN_COLS // bn),
        in_specs=[
            pl.BlockSpec((bm, D_MODEL), lambda i, j: (i, 0)),
            pl.BlockSpec((1, D_MODEL), lambda i, j: (0, 0)),
            pl.BlockSpec((D_MODEL, bn), lambda i, j: (0, j)),
        ],
        out_specs=pl.BlockSpec((bm, bn), lambda i, j: (i, j)),
        out_shape=jax.ShapeDtypeStruct((m, IN_COLS), BF16),
        scratch_shapes=[pltpu.VMEM((bm, D_MODEL), BF16)],
        compiler_params=pltpu.CompilerParams(
            dimension_semantics=("parallel", "arbitrary"), vmem_limit_bytes=VMEM_LIMIT_BYTES),
        name="inproj",
    )(x2d, g, w)


def _outproj_body(x_ref, ym_ref, yh_ref, ya_ref, wm_ref, wh_ref, wa_ref, fg_ref, o_ref, *, final):
    acc = x_ref[...] + _dot(ym_ref[...], wm_ref[...])
    acc = acc + _dot(yh_ref[...], wh_ref[...])
    acc = acc + _dot(ya_ref[...], wa_ref[...])
    if final:
        ms = jnp.mean(acc * acc, axis=-1, keepdims=True)
        acc = (acc * lax.rsqrt(ms + EPS)) * fg_ref[...]
    o_ref[...] = acc


def _outproj(x2d, ym, yh, ya, wm, wh, wa, fg, final):
    m = x2d.shape[0]
    bm = min(1024, m)
    km, kh, ka = ym.shape[1], yh.shape[1], ya.shape[1]
    full = lambda i: (0, 0)
    return pl.pallas_call(
        functools.partial(_outproj_body, final=final),
        grid=(m // bm,),
        in_specs=[
            pl.BlockSpec((bm, D_MODEL), lambda i: (i, 0)),
            pl.BlockSpec((bm, km), lambda i: (i, 0)),
            pl.BlockSpec((bm, kh), lambda i: (i, 0)),
            pl.BlockSpec((bm, ka), lambda i: (i, 0)),
            pl.BlockSpec((km, D_MODEL), full),
            pl.BlockSpec((kh, D_MODEL), full),
            pl.BlockSpec((ka, D_MODEL), full),
            pl.BlockSpec((1, D_MODEL), full),
        ],
        out_specs=pl.BlockSpec((bm, D_MODEL), lambda i: (i, 0)),
        out_shape=jax.ShapeDtypeStruct((m, D_MODEL), F32),
        input_output_aliases={0: 0},
        compiler_params=pltpu.CompilerParams(
            dimension_semantics=("parallel",), vmem_limit_bytes=VMEM_LIMIT_BYTES),
        name="outproj",
    )(x2d, ym, yh, ya, wm, wh, wa, fg)


def _rope(x, c, sa, sb):
    return x * c + pltpu.roll(x, LANES - ROPE_DIM // 2, 1) * sa + pltpu.roll(x, ROPE_DIM // 2, 1) * sb


def _attn_body(q_ref, k_ref, v_ref, z_ref, c_ref, sa_ref, sb_ref, lam_ref, ng_ref, o_ref, kr_ref,
               *, lam_init):
    i = pl.program_id(1)
    bq = q_ref.shape[0]

    @pl.when(i == 0)
    def _():
        c, sa, sb = c_ref[...], sa_ref[...], sb_ref[...]
        for p in range(A_HEADS):
            sl = slice(p * LANES, (p + 1) * LANES)
            kr_ref[:, sl] = _rope(k_ref[:, sl].astype(F32), c, sa, sb).astype(BF16)

    lp = lam_ref[...]
    lam = (jnp.exp(jnp.sum(lp[0:1] * lp[1:2], axis=-1, keepdims=True))
           - jnp.exp(jnp.sum(lp[2:3] * lp[3:4], axis=-1, keepdims=True)) + lam_init)

    r0 = pl.multiple_of(i * bq, bq)
    c = c_ref[pl.ds(r0, bq), :]
    sa = sa_ref[pl.ds(r0, bq), :]
    sb = sb_ref[pl.ds(r0, bq), :]
    lane = lax.broadcasted_iota(jnp.int32, (bq, LANES), 1)
    first = lane < A_QKDIM
    for p in range(A_HEADS):
        sl = slice(p * LANES, (p + 1) * LANES)
        qs = _rope(q_ref[:, sl].astype(F32), c, sa, sb) * (A_QKDIM ** -0.5)
        q1 = jnp.where(first, qs, 0.0).astype(BF16)
        q2 = jnp.where(first, 0.0, qs).astype(BF16)
        ks = kr_ref[:, sl]
        s1 = _nt(q1, ks)
        s2 = _nt(q2, ks)
        e1 = jnp.exp(s1 - jnp.max(s1, axis=-1, keepdims=True))
        e2 = jnp.exp(s2 - jnp.max(s2, axis=-1, keepdims=True))
        n1 = 1.0 / jnp.sum(e1, axis=-1, keepdims=True)
        n2 = lam / jnp.sum(e2, axis=-1, keepdims=True)
        w = (e1 * n1 - e2 * n2).astype(BF16)
        o = _dot(w, v_ref[:, sl])
        ms = jnp.mean(o * o, axis=-1, keepdims=True)
        y = (o * lax.rsqrt(ms + EPS)) * ng_ref[:, sl] * (1.0 - lam_init)
        o_ref[:, sl] = (y * _silu(z_ref[:, sl].astype(F32))).astype(o_ref.dtype)


def _attention(proj, rope_c, rope_sa, rope_sb, lam_p, ng, lam_init):
    b, s, _ = proj.shape
    bq = min(Q_BLOCK, s)
    col0 = (3 * M_WIDTH + 5 * H_WIDTH) // A_WIDTH
    seq_full = lambda c: pl.BlockSpec((None, s, A_WIDTH), lambda bi, i, c=c: (bi, 0, c))
    seq_blk = lambda c: pl.BlockSpec((None, bq, A_WIDTH), lambda bi, i, c=c: (bi, i, c))
    tab = pl.BlockSpec((None, s, LANES), lambda bi, i: (bi, 0, 0))
    return pl.pallas_call(
        functools.partial(_attn_body, lam_init=lam_init),
        grid=(b, s // bq),
        in_specs=[seq_blk(col0), seq_full(col0 + 1), seq_full(col0 + 2), seq_blk(col0 + 3),
                  tab, tab, tab,
                  pl.BlockSpec((4, A_QKDIM), lambda bi, i: (0, 0)),
                  pl.BlockSpec((1, A_WIDTH), lambda bi, i: (0, 0))],
        out_specs=pl.BlockSpec((None, bq, A_WIDTH), lambda bi, i: (bi, i, 0)),
        out_shape=jax.ShapeDtypeStruct((b, s, A_WIDTH), BF16),
        scratch_shapes=[pltpu.VMEM((s, A_WIDTH), BF16)],
        compiler_params=pltpu.CompilerParams(
            dimension_semantics=("parallel", "arbitrary"), vmem_limit_bytes=VMEM_LIMIT_BYTES),
        name="diff_attn",
    )(proj, proj, proj, proj, rope_c, rope_sa, rope_sb, lam_p, ng)


def _hgrn_consts():
    t = np.arange(H_CHUNK)
    tt, r = t[:, None], t[None, :]
    blocks = []
    for lvl in range(H_LEVELS):
        start = (t >> lvl) << lvl
        end = start + (1 << lvl) - 1
        is_right = ((t >> lvl) & 1) == 1
        prefix = (r >= start[:, None]) & (r <= tt)
        suffix = (r > tt) & (r <= end[:, None])
        blocks.append(np.where(is_right[:, None], prefix, suffix))
    blocks.append(r <= tt)
    blocks.append(r > tt)
    w_f = np.stack(blocks).astype(np.float32)
    w_b = w_f[:, ::-1, ::-1]
    x = tt ^ r
    top = np.where(x > 0, np.floor(np.log2(np.maximum(x, 1))), -1).astype(np.int32)
    lvl_f = np.where(tt > r, top, -1).astype(np.int32)
    nblk = H_LEVELS + 2
    return (w_f.reshape(nblk * H_CHUNK, H_CHUNK), np.ascontiguousarray(w_b).reshape(nblk * H_CHUNK, H_CHUNK),
            lvl_f, np.ascontiguousarray(lvl_f.T))


def _hgrn_chunk(q, k, g, v, w_ref, lvl, st_ref, fwd):
    T = H_CHUNK
    e = _sum_halves(_dot(w_ref[...], _split_bf16(g)))
    trow = lax.broadcasted_iota(jnp.int32, (T, 1), 0)
    scores = jnp.zeros((T, T), F32)
    for lv in range(H_LEVELS):
        q_role = ((trow >> lv) & 1) == (1 if fwd else 0)
        x = (jnp.where(q_role, q, k) * jnp.exp(e[lv * T:(lv + 1) * T])).astype(BF16)
        scores = jnp.where(lvl == lv, _nt(x, x), scores)
    diag = jnp.sum(q * k, axis=-1, keepdims=True)
    o = _dot(scores.astype(BF16), v) + diag * v.astype(F32)
    e_q = e[H_LEVELS * T:(H_LEVELS + 1) * T]
    e_k = e[(H_LEVELS + 1) * T:(H_LEVELS + 2) * T]
    st = st_ref[...]
    o = o + _nt((q * jnp.exp(e_q)).astype(BF16), st.astype(BF16))
    kt = (k * jnp.exp(e_k)).astype(BF16)
    e_tot = e_q[T - 1:T] if fwd else e_q[0:1]
    st_ref[...] = st * jnp.exp(e_tot) + _tn(v, kt)
    return o


def _hgrn_body(q_ref, ff_ref, fb_ref, i_ref, z_ref, lb_ref, ng_ref, wf_ref, wb_ref, lf_ref, lbm_ref,
               o_ref, k_s, g_s, acc_s, st_s, *, layer):
    s = q_ref.shape[0]
    n_chunks = s // H_CHUNK
    for d, pre_ref in enumerate((ff_ref, fb_ref)):
        z = lb_ref[d]
        p = jnp.exp(z - jnp.max(z, axis=0, keepdims=True))
        p = p / jnp.sum(p, axis=0, keepdims=True)
        lb = jnp.sum(p[1:layer + 1], axis=0, keepdims=True) if layer > 0 else jnp.zeros((1, LANES), F32)
        f = lb + (1.0 - lb) * _sigmoid(pre_ref[...].astype(F32))
        g_s[d] = jnp.log(f)
        k_s[d] = 1.0 - f
    acc_s[...] = jnp.zeros_like(acc_s)
    st_s[...] = jnp.zeros_like(st_s)
    lvl_f = lf_ref[...]
    lvl_b = lbm_ref[...]

    def step(n, carry):
        for d in range(2):
            c = n if d == 0 else n_chunks - 1 - n
            rows = pl.ds(pl.multiple_of(c * H_CHUNK, H_CHUNK), H_CHUNK)
            o = _hgrn_chunk(q_ref[rows, :].astype(F32), k_s[d, rows, :], g_s[d, rows, :], i_ref[rows, :],
                            wf_ref if d == 0 else wb_ref, lvl_f if d == 0 else lvl_b, st_s.at[d], d == 0)
            acc_s[rows, :] += o
        return carry

    lax.fori_loop(0, n_chunks, step, 0)
    o = acc_s[...]
    ms = jnp.mean(o * o, axis=-1, keepdims=True)
    y = (o * lax.rsqrt(ms + EPS)) * ng_ref[...]
    o_ref[...] = (y * _silu(z_ref[...].astype(F32))).astype(o_ref.dtype)


def _hgrn(proj, lb_logits, ng, layer):
    b, s, _ = proj.shape
    col0 = 3 * M_WIDTH // H_HEAD_DIM
    w_f, w_b, lvl_f, lvl_b = _hgrn_consts()
    seq = lambda g: pl.BlockSpec((None, s, H_HEAD_DIM), lambda bi, h, g=g: (bi, 0, col0 + g * H_HEADS + h))
    const = lambda shape: pl.BlockSpec(shape, lambda bi, h: (0,) * len(shape))
    return pl.pallas_call(
        functools.partial(_hgrn_body, layer=layer),
        grid=(b, H_HEADS),
        in_specs=[seq(0), seq(1), seq(2), seq(3), seq(4),
                  pl.BlockSpec((2, DEPTH, H_HEAD_DIM), lambda bi, h: (0, 0, h)),
                  pl.BlockSpec((1, H_HEAD_DIM), lambda bi, h: (0, h)),
                  const(w_f.shape), const(w_b.shape), const(lvl_f.shape), const(lvl_b.shape)],
        out_specs=pl.BlockSpec((None, s, H_HEAD_DIM), lambda bi, h: (bi, 0, h)),
        out_shape=jax.ShapeDtypeStruct((b, s, H_WIDTH), BF16),
        scratch_shapes=[pltpu.VMEM((2, s, H_HEAD_DIM), F32), pltpu.VMEM((2, s, H_HEAD_DIM), F32),
                        pltpu.VMEM((s, H_HEAD_DIM), F32), pltpu.VMEM((2, H_HEAD_DIM, H_HEAD_DIM), F32)],
        compiler_params=pltpu.CompilerParams(
            dimension_semantics=("parallel", "arbitrary"), vmem_limit_bytes=VMEM_LIMIT_BYTES),
        name="hgrn2",
    )(proj, proj, proj, proj, proj, jnp.swapaxes(lb_logits, 0, 1), ng,
      jnp.asarray(w_f, BF16), jnp.asarray(w_b, BF16), jnp.asarray(lvl_f), jnp.asarray(lvl_b))


def _log_sigmoid(x):
    return jnp.minimum(x, 0.0) - jnp.log(1.0 + jnp.exp(-jnp.abs(x)))


def _mlstm_chunk(q, k, vext, bc, br, ir, ic, m_prev, c_ref, fwd):
    T = M_CHUNK
    tt = lax.broadcasted_iota(jnp.int32, (T, T), 0)
    ss = lax.broadcasted_iota(jnp.int32, (T, T), 1)
    mask = (ss <= tt) if fwd else (ss >= tt)
    dm = jnp.where(mask, bc - br + ir, -jnp.inf)
    inter = bc + m_prev
    m_t = jnp.maximum(jnp.max(dm, axis=-1, keepdims=True), inter)
    w = jnp.exp(dm - m_t) * _nt(q, k)
    g_inter = jnp.exp(inter - m_t)
    cext = c_ref[...]
    nd = _dot(w.astype(BF16), vext) + g_inter * _dot(q, cext.astype(BF16))
    den = nd[:, M_HEAD_DIM:M_HEAD_DIM + 1]
    h = nd / jnp.maximum(jnp.abs(den), jnp.exp(-m_t))
    b_last = bc[T - 1:T] if fwd else bc[0:1]
    m_new = jnp.maximum(b_last + m_prev, jnp.max(b_last - br + ir, axis=-1, keepdims=True))
    decay = jnp.exp(b_last + m_prev - m_new)
    kt = (k.astype(F32) * jnp.exp(b_last - bc + ic - m_new)).astype(BF16)
    c_ref[...] = decay * cext + _tn(kt, vext)
    return h, m_new


def _mlstm_body(xm_ref, om_ref, zm_ref, cw_ref, cb_ref, wq_ref, wk_ref, wv_ref, gw_ref, gb_ref,
                skip_ref, ng_ref, tl_ref, tu_ref, o_ref, xc_s, q_s, k_s, v_s, g_s, h_s, c_s):
    s = xm_ref.shape[0]
    T = M_CHUNK
    n_chunks = s // T
    rb = min(256, s)
    halo = 16

    lane_w = lax.broadcasted_iota(jnp.int32, (rb, M_WIN), 1)
    for blk in range(s // rb):
        r0 = blk * rb
        mid = xm_ref[r0:r0 + rb, :].astype(F32)
        top = xm_ref[r0 - halo:r0, :].astype(F32) if blk > 0 else jnp.zeros((halo, M_WIDTH), F32)
        bot = (xm_ref[r0 + rb:r0 + rb + halo, :].astype(F32) if r0 + rb < s
               else jnp.zeros((halo, M_WIDTH), F32))
        xp = jnp.concatenate([top, mid, bot], axis=0)
        n_rows = rb + 2 * halo
        conv = jnp.zeros((rb, M_WIDTH), F32) + cb_ref[...]
        for j in range(M_CONV):
            shift = (M_CONV // 2 - j) % n_rows
            sh = xp if shift == 0 else pltpu.roll(xp, shift, 0)
            conv = conv + sh[halo:halo + rb] * cw_ref[j:j + 1, :]
        xc = _silu(conv)
        xc_s[r0:r0 + rb, :] = xc
        xcb = xc.astype(BF16)
        gates = jnp.zeros((rb, LANES), F32) + gb_ref[...]
        for h in range(M_HEADS):
            a = M_WIN_START[h]
            qh = _dot(xcb[:, a:a + M_WIN], wq_ref[h])
            kh = _dot(xcb[:, a:a + M_WIN], wk_ref[h])
            vh = _dot(xm_ref[r0:r0 + rb, a:a + M_WIN], wv_ref[h])
            qb, kb, vb = qh.astype(BF16), kh.astype(BF16), vh.astype(BF16)
            gates = gates + _dot(qb, gw_ref[0, h]) + _dot(kb, gw_ref[1, h]) + _dot(vb, gw_ref[2, h])
            q_s[h, r0:r0 + rb, :] = (qh * (M_HEAD_DIM ** -0.5)).astype(BF16)
            k_s[h, r0:r0 + rb, :] = kb
            v_s[h, r0:r0 + rb, :] = jnp.where(lane_w == M_HEAD_DIM, 1.0, vh).astype(BF16)
        g_s[r0:r0 + rb, :] = gates

    h_s[...] = jnp.zeros_like(h_s)
    c_s[...] = jnp.zeros_like(c_s)
    lane_g = lax.broadcasted_iota(jnp.int32, (T, LANES), 1)
    lane_h = lax.broadcasted_iota(jnp.int32, (T, M_WIN), 1)

    def step(n, ms):
        new_ms = []
        for d in range(2):
            fwd = d == 0
            c = n if fwd else n_chunks - 1 - n
            rows = pl.ds(pl.multiple_of(c * T, T), T)
            gt = g_s[rows, :]
            cum = _sum_halves(_dot(tl_ref[...] if fwd else tu_ref[...], _split_bf16(_log_sigmoid(gt))))
            f0 = M_HEADS if fwd else 3 * M_HEADS
            col = jnp.where((lane_g >= f0) & (lane_g < f0 + M_HEADS), cum, gt)
            row = col.T
            i0 = 0 if fwd else 2 * M_HEADS
            for h in range(M_HEADS):
                bc, ic = col[:, f0 + h:f0 + h + 1], col[:, i0 + h:i0 + h + 1]
                br, ir = row[f0 + h:f0 + h + 1, :], row[i0 + h:i0 + h + 1, :]
                idx = d * M_HEADS + h
                hh, m_new = _mlstm_chunk(q_s[h, rows, :], k_s[h, rows, :], v_s[h, rows, :],
                                         bc, br, ir, ic, ms[idx], c_s.at[idx], fwd)
                hh = jnp.where(lane_h < M_HEAD_DIM, hh, 0.0)
                if M_WIN_OFF[h]:
                    hh = pltpu.roll(hh, M_WIN_OFF[h], 1)
                h_s[h, rows, :] += hh
                new_ms.append(m_new)
        return tuple(new_ms)

    lax.fori_loop(0, n_chunks, step, tuple(jnp.zeros((1, 1), F32) for _ in range(2 * M_HEADS)))

    for blk in range(s // rb):
        r0 = blk * rb
        for h in range(M_HEADS):
            a = M_WIN_START[h]
            hm = _sigmoid(om_ref[r0:r0 + rb, a:a + M_WIN].astype(F32)) * h_s[h, r0:r0 + rb, :]
            ms = jnp.sum(hm * hm, axis=-1, keepdims=True) * (1.0 / M_HEAD_DIM)
            y = (hm * lax.rsqrt(ms + EPS)) * ng_ref[h] + skip_ref[h] * xc_s[r0:r0 + rb, a:a + M_WIN]
            y = y * _silu(zm_ref[r0:r0 + rb, a:a + M_WIN].astype(F32))
            o_ref[r0:r0 + rb, h * M_WIN:(h + 1) * M_WIN] = y.astype(o_ref.dtype)


def _mlstm(proj, cw, cb, wq, wk, wv, gw, gb, skip, ng):
    b, s, _ = proj.shape
    tri = np.tril(np.ones((M_CHUNK, M_CHUNK), np.float32))
    seq = lambda c: pl.BlockSpec((None, s, M_WIDTH), lambda bi, c=c: (bi, 0, c), pipeline_mode=pl.Buffered(1))
    const = lambda shape: pl.BlockSpec(shape, lambda bi: (0,) * len(shape))
    args = (cw, cb, wq, wk, wv, gw, gb, skip, ng, jnp.asarray(tri, BF16), jnp.asarray(tri.T, BF16))
    return pl.pallas_call(
        _mlstm_body,
        grid=(b,),
        in_specs=[seq(0), seq(1), seq(2)] + [const(a.shape) for a in args],
        out_specs=pl.BlockSpec((None, s, M_HEADS * M_WIN), lambda bi: (bi, 0, 0)),
        out_shape=jax.ShapeDtypeStruct((b, s, M_HEADS * M_WIN), BF16),
        scratch_shapes=[
            pltpu.VMEM((s, M_WIDTH), F32),
            pltpu.VMEM((M_HEADS, s, M_WIN), BF16),
            pltpu.VMEM((M_HEADS, s, M_WIN), BF16),
            pltpu.VMEM((M_HEADS, s, M_WIN), BF16),
            pltpu.VMEM((s, LANES), F32),
            pltpu.VMEM((M_HEADS, s, M_WIN), F32),
            pltpu.VMEM((2 * M_HEADS, M_WIN, M_WIN), F32),
        ],
        compiler_params=pltpu.CompilerParams(
            dimension_semantics=("parallel",), vmem_limit_bytes=VMEM_LIMIT_BYTES),
        name="mlstm",
    )(proj, proj, proj, *args)


def _mlstm_params(m_conv_w, m_conv_b, m_wq, m_wk, m_wv, m_w_gates, m_b_gates, m_skip, m_norm_g, w_out_m):
    def dense(w):
        eye = jnp.eye(M_WIDTH // M_QKV_BLOCK, dtype=F32)
        return jnp.einsum("gh,gio->giho", eye, w).reshape(M_WIDTH, M_WIDTH)

    def head_weights(w):
        d = dense(w)
        out = []
        for h in range(M_HEADS):
            blk = d[M_WIN_START[h]:M_WIN_START[h] + M_WIN, h * M_HEAD_DIM:(h + 1) * M_HEAD_DIM]
            out.append(jnp.pad(blk, ((0, 0), (0, M_WIN - M_HEAD_DIM))))
        return jnp.stack(out).astype(BF16)

    def window(vec):
        out = []
        for h in range(M_HEADS):
            head = vec[h * M_HEAD_DIM:(h + 1) * M_HEAD_DIM]
            out.append(jnp.pad(head, (M_WIN_OFF[h], M_WIN - M_HEAD_DIM - M_WIN_OFF[h])))
        return jnp.stack(out)[:, None, :]

    gw = m_w_gates.reshape(3, M_HEADS, M_HEAD_DIM, 4 * M_HEADS)
    gw = jnp.pad(gw, ((0, 0), (0, 0), (0, M_WIN - M_HEAD_DIM), (0, LANES - 4 * M_HEADS))).astype(BF16)
    gb = jnp.pad(m_b_gates, (0, LANES - 4 * M_HEADS))[None, :]
    wo = []
    for h in range(M_HEADS):
        rows = w_out_m[h * M_HEAD_DIM:(h + 1) * M_HEAD_DIM]
        wo.append(jnp.pad(rows, ((M_WIN_OFF[h], M_WIN - M_HEAD_DIM - M_WIN_OFF[h]), (0, 0))))
    return (m_conv_w, m_conv_b[None, :], head_weights(m_wq), head_weights(m_wk), head_weights(m_wv),
            gw, gb, window(m_skip), window(m_norm_g)), jnp.concatenate(wo, axis=0).astype(BF16)


def _rope_tables(positions):
    half = ROPE_DIM // 2
    inv = ROPE_THETA ** (-jnp.arange(half, dtype=F32) / half)
    ang = positions.astype(F32)[..., None] * inv
    cos, sin = jnp.cos(ang), jnp.sin(ang)
    shp = positions.shape
    one = jnp.ones(shp + (A_QKDIM - ROPE_DIM,), F32)
    zero = lambda n: jnp.zeros(shp + (n,), F32)
    c = jnp.concatenate([cos, cos, one], axis=-1)
    sa = jnp.concatenate([-sin, zero(A_QKDIM - half)], axis=-1)
    sb = jnp.concatenate([zero(half), sin, zero(A_QKDIM - ROPE_DIM)], axis=-1)
    two = lambda t: jnp.concatenate([t, t], axis=-1)
    return two(c), two(sa), two(sb)


def kernel(x, positions, norm_g, w_in, m_conv_w, m_conv_b, m_wq, m_wk, m_wv, m_w_gates, m_b_gates,
           m_skip, m_norm_g, h_lb_logits, h_norm_g, a_lambda, a_norm_g, w_out, final_g):
    b, s, _ = x.shape
    rope_c, rope_sa, rope_sb = _rope_tables(positions)
    w_in_b = w_in.astype(BF16)
    w_out_b = w_out.astype(BF16)
    x2d = x.reshape(b * s, D_MODEL)
    for l in range(DEPTH):
        proj = _inproj(x2d, norm_g[l][None, :], w_in_b[l]).reshape(b, s, IN_COLS)
        m_args, w_out_m = _mlstm_params(m_conv_w[l], m_conv_b[l], m_wq[l], m_wk[l], m_wv[l], m_w_gates[l],
                                        m_b_gates[l], m_skip[l], m_norm_g[l], w_out[l, :M_WIDTH])
        y_m = _mlstm(proj, *m_args)
        y_h = _hgrn(proj, h_lb_logits, h_norm_g[l][None, :], l)
        lam_init = 0.8 - 0.6 * math.exp(-0.3 * l)
        y_a = _attention(proj, rope_c, rope_sa, rope_sb, a_lambda[l], a_norm_g[l][None, :], lam_init)
        x2d = _outproj(x2d, y_m.reshape(b * s, -1), y_h.reshape(b * s, -1), y_a.reshape(b * s, -1),
                       w_out_m, w_out_b[l, M_WIDTH:M_WIDTH + H_WIDTH], w_out_b[l, M_WIDTH + H_WIDTH:],
                       final_g[None, :], l == DEPTH - 1)
    return x2d.reshape(b, s, D_MODEL)
```

```python
import functools
import math

import numpy as np
import jax
import jax.numpy as jnp
from jax import lax
from jax.experimental import pallas as pl
from jax.experimental.pallas import tpu as pltpu

F32 = jnp.float32
BF16 = jnp.bfloat16

D_MODEL = 1024
DEPTH = 4
M_WIDTH = 768
M_HEADS = 4
M_HEAD_DIM = 192
M_QKV_BLOCK = 4
M_CONV = 5
H_WIDTH = 768
H_HEAD_DIM = 128
H_HEADS = 6
A_WIDTH = 512
A_HEADS = 4
A_VDIM = 128
A_QKDIM = 64
ROPE_DIM = 16
ROPE_THETA = 500000.0
IN_COLS = 3 * M_WIDTH + 5 * H_WIDTH + 4 * A_WIDTH
EPS = 1e-6

LANES = 128
MXU_DIM = 256
VMEM_LIMIT_BYTES = 56 * 1024 * 1024

M_WIN = MXU_DIM
M_WIN_START = (0, 128, 384, 512)
M_WIN_OFF = (0, 64, 0, 64)
M_CHUNK = 128
M_ROW_BLOCK = 256
H_CHUNK = 128
H_LEVELS = 7
H_SUB = 64
H_SAFE_LOG_DECAY = -80.0
H_BLOCKS_PER_STEP = 2
M_DEN_LANE = (192, 0, 192, 0)
Q_BLOCK = 256


def _nt(a, b):
    return lax.dot_general(a, b, (((1,), (1,)), ((), ())), preferred_element_type=F32)


def _tn(a, b):
    return lax.dot_general(a, b, (((0,), (0,)), ((), ())), preferred_element_type=F32)


def _dot(a, b):
    return jnp.dot(a, b, preferred_element_type=F32)


def _sigmoid(x):
    return 1.0 / (1.0 + jnp.exp(-x))


def _silu(x):
    return x * _sigmoid(x)


def _split_bf16(x):
    hi = x.astype(BF16)
    lo = (x - hi.astype(F32)).astype(BF16)
    return jnp.concatenate([hi, lo], axis=1)


def _sum_halves(r):
    return r[:, :LANES] + r[:, LANES:]


def _inproj_body(x_ref, g_ref, w_ref, o_ref, h_ref):
    @pl.when(pl.program_id(1) == 0)
    def _():
        xf = x_ref[...]
        ms = jnp.mean(xf * xf, axis=-1, keepdims=True)
        h_ref[...] = ((xf * lax.rsqrt(ms + EPS)) * g_ref[...]).astype(BF16)

    o_ref[...] = _dot(h_ref[...], w_ref[...]).astype(o_ref.dtype)


def _inproj(x2d, g, w):
    m = x2d.shape[0]
    bm = min(1024, m)
    bn = 2048
    return pl.pallas_call(
        _inproj_body,
        grid=(m // bm, IN_COLS // bn),
        in_specs=[
            pl.BlockSpec((bm, D_MODEL), lambda i, j: (i, 0)),
            pl.BlockSpec((1, D_MODEL), lambda i, j: (0, 0)),
            pl.BlockSpec((D_MODEL, bn), lambda i, j: (0, j)),
        ],
        out_specs=pl.BlockSpec((bm, bn), lambda i, j: (i, j)),
        out_shape=jax.ShapeDtypeStruct((m, IN_COLS), BF16),
        scratch_shapes=[pltpu.VMEM((bm, D_MODEL), BF16)],
        compiler_params=pltpu.CompilerParams(
            dimension_semantics=("parallel", "arbitrary"), vmem_limit_bytes=VMEM_LIMIT_BYTES),
        name="inproj",
    )(x2d, g, w)


def _outproj_body(x_ref, ym_ref, yh_ref, ya_ref, wm_ref, wh_ref, wa_ref, fg_ref, o_ref, *, final):
    acc = x_ref[...] + _dot(ym_ref[...], wm_ref[...])
    acc = acc + _dot(yh_ref[...], wh_ref[...])
    acc = acc + _dot(ya_ref[...], wa_ref[...])
    if final:
        ms = jnp.mean(acc * acc, axis=-1, keepdims=True)
        acc = (acc * lax.rsqrt(ms + EPS)) * fg_ref[...]
    o_ref[...] = acc


def _outproj(x2d, ym, yh, ya, wm, wh, wa, fg, final):
    m = x2d.shape[0]
    bm = min(1024, m)
    km, kh, ka = ym.shape[1], yh.shape[1], ya.shape[1]
    full = lambda i: (0, 0)
    return pl.pallas_call(
        functools.partial(_outproj_body, final=final),
        grid=(m // bm,),
        in_specs=[
            pl.BlockSpec((bm, D_MODEL), lambda i: (i, 0)),
            pl.BlockSpec((bm, km), lambda i: (i, 0)),
            pl.BlockSpec((bm, kh), lambda i: (i, 0)),
            pl.BlockSpec((bm, ka), lambda i: (i, 0)),
            pl.BlockSpec((km, D_MODEL), full),
            pl.BlockSpec((kh, D_MODEL), full),
            pl.BlockSpec((ka, D_MODEL), full),
            pl.BlockSpec((1, D_MODEL), full),
        ],
        out_specs=pl.BlockSpec((bm, D_MODEL), lambda i: (i, 0)),
        out_shape=jax.ShapeDtypeStruct((m, D_MODEL), F32),
        input_output_aliases={0: 0},
        compiler_params=pltpu.CompilerParams(
            dimension_semantics=("parallel",), vmem_limit_bytes=VMEM_LIMIT_BYTES),
        name="outproj",
    )(x2d, ym, yh, ya, wm, wh, wa, fg)


def _rope(x, c, sa, sb):
    return x * c + pltpu.roll(x, LANES - ROPE_DIM // 2, 1) * sa + pltpu.roll(x, ROPE_DIM // 2, 1) * sb


def _attn_body(q_ref, k_ref, v_ref, z_ref, c_ref, sa_ref, sb_ref, lam_ref, ng_ref, o_ref, kr_ref, vt_ref,
               *, lam_init):
    i = pl.program_id(1)
    bq = q_ref.shape[0]
    s = k_ref.shape[0]

    @pl.when(i == 0)
    def _():
        c, sa, sb = c_ref[...], sa_ref[...], sb_ref[...]
        for p in range(A_HEADS):
            sl = slice(p * LANES, (p + 1) * LANES)
            kr_ref[:, sl] = _rope(k_ref[:, sl].astype(F32), c, sa, sb).astype(BF16)
            for blk in range(s // bq):
                rows = slice(blk * bq, (blk + 1) * bq)
                vt_ref[p, :, rows] = v_ref[rows, sl].astype(F32).T.astype(BF16)

    lp = lam_ref[...]
    lam = (jnp.exp(jnp.sum(lp[0:1] * lp[1:2], axis=-1, keepdims=True))
           - jnp.exp(jnp.sum(lp[2:3] * lp[3:4], axis=-1, keepdims=True)) + lam_init)

    r0 = pl.multiple_of(i * bq, bq)
    c = c_ref[pl.ds(r0, bq), :]
    sa = sa_ref[pl.ds(r0, bq), :]
    sb = sb_ref[pl.ds(r0, bq), :]
    first = lax.broadcasted_iota(jnp.int32, (LANES, bq), 0) < A_QKDIM

    def scores(p):
        sl = slice(p * LANES, (p + 1) * LANES)
        qt = (_rope(q_ref[:, sl].astype(F32), c, sa, sb) * (A_QKDIM ** -0.5 * math.log2(math.e))).T
        ks = kr_ref[:, sl]
        return (_dot(ks, jnp.where(first, qt, 0.0).astype(BF16)),
                _dot(ks, jnp.where(first, 0.0, qt).astype(BF16)))

    nxt = scores(0)
    for p in range(A_HEADS):
        sl = slice(p * LANES, (p + 1) * LANES)
        s1, s2 = nxt
        if p + 1 < A_HEADS:
            nxt = scores(p + 1)
        e1 = jnp.exp2(s1 - jnp.max(s1, axis=0, keepdims=True))
        e2 = jnp.exp2(s2 - jnp.max(s2, axis=0, keepdims=True))
        n1 = 1.0 / jnp.sum(e1, axis=0, keepdims=True)
        n2 = lam / jnp.sum(e2, axis=0, keepdims=True)
        ot = _dot(vt_ref[p], e1.astype(BF16)) * n1 - _dot(vt_ref[p], e2.astype(BF16)) * n2
        o = ot.T
        ms = jnp.mean(o * o, axis=-1, keepdims=True)
        y = (o * lax.rsqrt(ms + EPS)) * ng_ref[:, sl] * (1.0 - lam_init)
        o_ref[:, sl] = (y * _silu(z_ref[:, sl].astype(F32))).astype(o_ref.dtype)


def _attention(proj, rope_c, rope_sa, rope_sb, lam_p, ng, lam_init):
    b, s, _ = proj.shape
    bq = min(Q_BLOCK, s)
    col0 = (3 * M_WIDTH + 5 * H_WIDTH) // A_WIDTH
    seq_full = lambda c: pl.BlockSpec((None, s, A_WIDTH), lambda bi, i, c=c: (bi, 0, c))
    seq_blk = lambda c: pl.BlockSpec((None, bq, A_WIDTH), lambda bi, i, c=c: (bi, i, c))
    tab = pl.BlockSpec((None, s, LANES), lambda bi, i: (bi, 0, 0))
    return pl.pallas_call(
        functools.partial(_attn_body, lam_init=lam_init),
        grid=(b, s // bq),
        in_specs=[seq_blk(col0), seq_full(col0 + 1), seq_full(col0 + 2), seq_blk(col0 + 3),
                  tab, tab, tab,
                  pl.BlockSpec((4, A_QKDIM), lambda bi, i: (0, 0)),
                  pl.BlockSpec((1, A_WIDTH), lambda bi, i: (0, 0))],
        out_specs=pl.BlockSpec((None, bq, A_WIDTH), lambda bi, i: (bi, i, 0)),
        out_shape=jax.ShapeDtypeStruct((b, s, A_WIDTH), BF16),
        scratch_shapes=[pltpu.VMEM((s, A_WIDTH), BF16),
                        pltpu.VMEM((A_HEADS, A_VDIM, s), BF16)],
        compiler_params=pltpu.CompilerParams(
            dimension_semantics=("parallel", "arbitrary"), vmem_limit_bytes=VMEM_LIMIT_BYTES),
        name="diff_attn",
    )(proj, proj, proj, proj, rope_c, rope_sa, rope_sb, lam_p, ng)


def _hgrn_consts():
    t = np.arange(H_CHUNK)
    tt, r = t[:, None], t[None, :]
    blocks = []
    for lvl in range(H_LEVELS):
        start = (t >> lvl) << lvl
        end = start + (1 << lvl) - 1
        is_right = ((t >> lvl) & 1) == 1
        prefix = (r >= start[:, None]) & (r <= tt)
        suffix = (r > tt) & (r <= end[:, None])
        blocks.append(np.where(is_right[:, None], prefix, suffix))
    blocks.append(r <= tt)
    blocks.append(r > tt)
    w_f = np.stack(blocks).astype(np.float32)
    w_b = w_f[:, ::-1, ::-1]
    x = tt ^ r
    top = np.where(x > 0, np.floor(np.log2(np.maximum(x, 1))), -1).astype(np.int32)
    lvl_f = np.where(tt > r, top, -1).astype(np.int32)
    nblk = H_LEVELS + 2
    return (w_f.reshape(nblk * H_CHUNK, H_CHUNK), np.ascontiguousarray(w_b).reshape(nblk * H_CHUNK, H_CHUNK),
            lvl_f, np.ascontiguousarray(lvl_f.T))


def _hgrn_chunk(q, k, g, v, vt, w_ref, lvl, st_ref, fwd):
    T = H_CHUNK
    e = _sum_halves(_dot(w_ref[...], _split_bf16(g)))
    trow = lax.broadcasted_iota(jnp.int32, (T, 1), 0)
    scores = jnp.zeros((T, T), F32)
    for lv in range(H_LEVELS):
        q_role = ((trow >> lv) & 1) == (1 if fwd else 0)
        x = (jnp.where(q_role, q, k) * jnp.exp(e[lv * T:(lv + 1) * T])).astype(BF16)
        scores = jnp.where(lvl == lv, _nt(x, x), scores)
    diag = jnp.sum(q * k, axis=-1, keepdims=True)
    o = _dot(scores.astype(BF16), v) + diag * v.astype(F32)
    e_q = e[H_LEVELS * T:(H_LEVELS + 1) * T]
    e_k = e[(H_LEVELS + 1) * T:(H_LEVELS + 2) * T]
    st = st_ref[...]
    o = o + _nt((q * jnp.exp(e_q)).astype(BF16), st.astype(BF16))
    kt = (k * jnp.exp(e_k)).astype(BF16)
    e_tot = e_q[T - 1:T] if fwd else e_q[0:1]
    st_ref[...] = st * jnp.exp(e_tot) + _dot(vt, kt)
    return o


def _hgrn_fast_blocks(chains):
    T = H_CHUNK
    trow = lax.broadcasted_iota(jnp.int32, (T, 1), 0)
    cums = [_sum_halves(_dot(tri, _split_bf16(g))) for (_, _, g, _, _, tri, _, _) in chains]
    qts, kfs, decays = [], [], []
    for (q, k, _, _, _, _, _, fwd), cum in zip(chains, cums):
        qts.append((q * jnp.exp(cum)).astype(BF16))
        kfs.append(k * jnp.exp(-cum))
        edges = [(a + 1) * H_SUB - 1 if fwd else a * H_SUB for a in range(T // H_SUB)]
        decays.append([jnp.exp(cum[e:e + 1]) for e in edges])
    scores = [_nt(qt, kf.astype(BF16)) for qt, kf in zip(qts, kfs)]
    incs = []
    for (_, _, _, _, vt, _, _, _), kf, dec in zip(chains, kfs, decays):
        per_sub = []
        for a, e_tot in enumerate(dec):
            in_sub = (trow >= a * H_SUB) & (trow < (a + 1) * H_SUB)
            per_sub.append(_dot(vt, jnp.where(in_sub, kf * e_tot, 0.0).astype(BF16)))
        incs.append(per_sub)
    outs = [_dot(jnp.where(mask, sc, 0.0).astype(BF16), v)
            for (_, _, _, v, _, _, mask, _), sc in zip(chains, scores)]
    return outs, qts, incs, decays


def _hgrn_body(q_ref, ff_ref, fb_ref, i_ref, z_ref, lb_ref, ng_ref, wf_ref, wb_ref, lf_ref, lbm_ref,
               tf_ref, tb_ref, mf_ref, mb_ref, o_ref, k_s, g_s, acc_s, st_s, vt_s, qt_s, inc_s, dec_s,
               *, layer):
    s = q_ref.shape[0]
    n_chunks = s // H_CHUNK
    n_sub = s // H_SUB
    sub_per_chunk = H_CHUNK // H_SUB
    lowest = None
    for d, pre_ref in enumerate((ff_ref, fb_ref)):
        z = lb_ref[d]
        p = jnp.exp(z - jnp.max(z, axis=0, keepdims=True))
        p = p / jnp.sum(p, axis=0, keepdims=True)
        lb = jnp.sum(p[1:layer + 1], axis=0, keepdims=True) if layer > 0 else jnp.zeros((1, LANES), F32)
        f = lb + (1.0 - lb) * _sigmoid(pre_ref[...].astype(F32))
        g = jnp.log(f)
        g_s[d] = g
        k_s[d] = 1.0 - f
        tot = jnp.min(jnp.sum(g.reshape(n_sub, H_SUB, LANES), axis=1))
        lowest = tot if lowest is None else jnp.minimum(lowest, tot)
    for c in range(n_chunks):
        vt_s[c] = i_ref[c * H_CHUNK:(c + 1) * H_CHUNK, :].astype(F32).T.astype(BF16)

    def block_args(c, d):
        rows = pl.ds(pl.multiple_of(c * H_CHUNK, H_CHUNK), H_CHUNK)
        return rows, (q_ref[rows, :].astype(F32), k_s[d, rows, :], g_s[d, rows, :], i_ref[rows, :], vt_s[c])

    def fast():
        tris = (tf_ref[...], tb_ref[...])
        masks = (mf_ref[...] > 0, mb_ref[...] > 0)

        def intra(i, carry):
            where, chains = [], []
            for u in range(H_BLOCKS_PER_STEP):
                c = i * H_BLOCKS_PER_STEP + u
                for d in range(2):
                    rows, args = block_args(c, d)
                    where.append((c, d, rows))
                    chains.append(args + (tris[d], masks[d], d == 0))
            outs, qts, incs, decays = _hgrn_fast_blocks(chains)
            for (c, d, rows), o, qt, inc, dec in zip(where, outs, qts, incs, decays):
                if d == 0:
                    acc_s[rows, :] = o
                else:
                    acc_s[rows, :] += o
                qt_s[d, rows, :] = qt
                for a in range(sub_per_chunk):
                    inc_s[d, c * sub_per_chunk + a] = inc[a]
                    dec_s[d, c * sub_per_chunk + a] = dec[a]
            return carry

        lax.fori_loop(0, n_chunks // H_BLOCKS_PER_STEP, intra, 0)

        def scan(j, states):
            new = []
            for d in range(2):
                c = j if d == 0 else n_sub - 1 - j
                rows = pl.ds(pl.multiple_of(c * H_SUB, H_SUB), H_SUB)
                acc_s[rows, :] += _nt(qt_s[d, rows, :], states[d].astype(BF16))
                new.append(states[d] * dec_s[d, c] + inc_s[d, c])
            return tuple(new)

        zero = jnp.zeros((H_HEAD_DIM, H_HEAD_DIM), F32)
        lax.fori_loop(0, n_sub, scan, (zero, zero), unroll=4)

    def safe():
        acc_s[...] = jnp.zeros_like(acc_s)
        st_s[...] = jnp.zeros_like(st_s)
        lvl_f = lf_ref[...]
        lvl_b = lbm_ref[...]

        def step(n, carry):
            for d in range(2):
                rows, args = block_args(n if d == 0 else n_chunks - 1 - n, d)
                acc_s[rows, :] += _hgrn_chunk(*args, wf_ref if d == 0 else wb_ref,
                                              lvl_f if d == 0 else lvl_b, st_s.at[d], d == 0)
            return carry

        lax.fori_loop(0, n_chunks, step, 0)

    lax.cond(lowest >= H_SAFE_LOG_DECAY, fast, safe)
    o = acc_s[...]
    ms = jnp.mean(o * o, axis=-1, keepdims=True)
    y = (o * lax.rsqrt(ms + EPS)) * ng_ref[...]
    o_ref[...] = (y * _silu(z_ref[...].astype(F32))).astype(o_ref.dtype)


def _hgrn(proj, lb_logits, ng, layer):
    b, s, _ = proj.shape
    col0 = 3 * M_WIDTH // H_HEAD_DIM
    w_f, w_b, lvl_f, lvl_b = _hgrn_consts()
    tri_f, tri_b = _hgrn_fast_consts()
    seq = lambda g: pl.BlockSpec((None, s, H_HEAD_DIM), lambda bi, h, g=g: (bi, 0, col0 + g * H_HEADS + h))
    const = lambda shape: pl.BlockSpec(shape, lambda bi, h: (0,) * len(shape))
    consts = (jnp.asarray(w_f, BF16), jnp.asarray(w_b, BF16), jnp.asarray(lvl_f), jnp.asarray(lvl_b),
              jnp.asarray(tri_f, BF16), jnp.asarray(tri_b, BF16),
              jnp.asarray(tri_f, jnp.int32), jnp.asarray(tri_b, jnp.int32))
    n_sub = s // H_SUB
    return pl.pallas_call(
        functools.partial(_hgrn_body, layer=layer),
        grid=(b, H_HEADS),
        in_specs=[seq(0), seq(1), seq(2), seq(3), seq(4),
                  pl.BlockSpec((2, DEPTH, H_HEAD_DIM), lambda bi, h: (0, 0, h)),
                  pl.BlockSpec((1, H_HEAD_DIM), lambda bi, h: (0, h))] + [const(c.shape) for c in consts],
        out_specs=pl.BlockSpec((None, s, H_HEAD_DIM), lambda bi, h: (bi, 0, h)),
        out_shape=jax.ShapeDtypeStruct((b, s, H_WIDTH), BF16),
        scratch_shapes=[pltpu.VMEM((2, s, H_HEAD_DIM), F32),
                        pltpu.VMEM((2, s, H_HEAD_DIM), F32),
                        pltpu.VMEM((s, H_HEAD_DIM), F32),
                        pltpu.VMEM((2, H_HEAD_DIM, H_HEAD_DIM), F32),
                        pltpu.VMEM((s // H_CHUNK, H_HEAD_DIM, H_CHUNK), BF16),
                        pltpu.VMEM((2, s, H_HEAD_DIM), BF16),
                        pltpu.VMEM((2, n_sub, H_HEAD_DIM, H_HEAD_DIM), F32),
                        pltpu.VMEM((2, n_sub, 1, H_HEAD_DIM), F32)],
        compiler_params=pltpu.CompilerParams(
            dimension_semantics=("parallel", "arbitrary"), vmem_limit_bytes=VMEM_LIMIT_BYTES),
        name="hgrn2",
    )(proj, proj, proj, proj, proj, jnp.swapaxes(lb_logits, 0, 1), ng, *consts)


def _hgrn_fast_consts():
    t = np.arange(H_CHUNK)
    tt, r = t[:, None], t[None, :]
    same = (tt // H_SUB) == (r // H_SUB)
    return (same & (r <= tt)).astype(np.float32), (same & (r >= tt)).astype(np.float32)


def _log_sigmoid(x):
    return jnp.minimum(x, 0.0) - jnp.log(1.0 + jnp.exp(-jnp.abs(x)))


def _mlstm_chunks(chains, gates_of):
    T = M_CHUNK
    cexts = [c_ref[...] for (_, _, _, _, _, c_ref, _, _) in chains]
    sc = [_dot(q, kt) for (q, kt, _, _, _, _, _, _) in chains]
    qc = [_dot(q, cext.astype(BF16)) for (q, _, _, _, _, _, _, _), cext in zip(chains, cexts)]
    gates = gates_of()
    upd, state = [], []
    for (_, kt, vext, _, m_prev, _, fwd, _), (bc, br, ir) in zip(chains, gates):
        b_last = bc[T - 1:T] if fwd else bc[0:1]
        dec = b_last - (br - ir)
        m_new = jnp.maximum(b_last + m_prev, jnp.max(dec, axis=-1, keepdims=True))
        upd.append(_dot((kt.astype(F32) * jnp.exp(dec - m_new)).astype(BF16), vext))
        state.append((jnp.exp(b_last + m_prev - m_new), m_new))
    ws, scale = [], []
    for (_, _, _, mask, m_prev, _, _, _), (bc, br, ir), s_qk in zip(chains, gates, sc):
        a = jnp.where(mask, ir - br, -jnp.inf)
        mu = jnp.maximum(jnp.max(a, axis=-1, keepdims=True), m_prev)
        ws.append((jnp.exp(a - mu) * s_qk).astype(BF16))
        scale.append((jnp.exp(m_prev - mu), jnp.exp(-(bc + mu))))
    out = []
    for chain, cext, w, q_c, u, (g_inter, floor), (decay, m_new) in zip(chains, cexts, ws, qc, upd, scale, state):
        vext, c_ref, den_lane = chain[2], chain[5], chain[7]
        nd = _dot(w, vext) + g_inter * q_c
        den = nd[:, den_lane:den_lane + 1]
        c_ref[...] = decay * cext + u
        out.append((nd / jnp.maximum(jnp.abs(den), floor), m_new))
    return out


def _mlstm_body(xm_ref, om_ref, zm_ref, cw_ref, cb_ref, wq_ref, wk_ref, wkt_ref, wv_ref, gw_ref, gwt_ref,
                gb_ref, gbt_ref, skip_ref, ng_ref, tl_ref, tu_ref, o_ref,
                xc_s, q_s, kt_s, v_s, g_s, gr_s, h_s, c_s):
    s = xm_ref.shape[0]
    T = M_CHUNK
    n_chunks = s // T
    rb = min(M_ROW_BLOCK, s)
    halo = 16
    n_gates = 4 * M_HEADS

    lane_w = lax.broadcasted_iota(jnp.int32, (rb, M_WIN), 1)
    lane_g = lax.broadcasted_iota(jnp.int32, (rb, LANES), 1)
    gate_id = lax.broadcasted_iota(jnp.int32, (n_gates, rb), 0)
    for blk in range(s // rb):
        r0 = blk * rb
        mid = xm_ref[r0:r0 + rb, :].astype(F32)
        top = xm_ref[r0 - halo:r0, :].astype(F32) if blk > 0 else jnp.zeros((halo, M_WIDTH), F32)
        bot = (xm_ref[r0 + rb:r0 + rb + halo, :].astype(F32) if r0 + rb < s
               else jnp.zeros((halo, M_WIDTH), F32))
        xp = jnp.concatenate([top, mid, bot], axis=0)
        n_rows = rb + 2 * halo
        conv = jnp.zeros((rb, M_WIDTH), F32) + cb_ref[...]
        for j in range(M_CONV):
            shift = (M_CONV // 2 - j) % n_rows
            sh = xp if shift == 0 else pltpu.roll(xp, shift, 0)
            conv = conv + sh[halo:halo + rb] * cw_ref[j:j + 1, :]
        xc = _silu(conv)
        xc_s[r0:r0 + rb, :] = xc
        xcb = xc.astype(BF16)
        gates = jnp.zeros((rb, LANES), F32) + gb_ref[...]
        gates_t = jnp.zeros((n_gates, rb), F32) + gbt_ref[...]
        for h in range(M_HEADS):
            a = M_WIN_START[h]
            xw = xcb[:, a:a + M_WIN]
            qh = _dot(xw, wq_ref[h])
            kh = _dot(xw, wk_ref[h])
            kht = _nt(wkt_ref[h], xw)
            vh = _dot(xm_ref[r0:r0 + rb, a:a + M_WIN], wv_ref[h])
            qb, kb, vb = qh.astype(BF16), kh.astype(BF16), vh.astype(BF16)
            gates = gates + _dot(qb, gw_ref[0, h]) + _dot(kb, gw_ref[1, h]) + _dot(vb, gw_ref[2, h])
            gates_t = gates_t + _nt(gwt_ref[0, h], qb) + _nt(gwt_ref[1, h], kb) + _nt(gwt_ref[2, h], vb)
            q_s[h, r0:r0 + rb, :] = (qh * (M_HEAD_DIM ** -0.5)).astype(BF16)
            for a_c in range(rb // T):
                kt_s[h, r0 // T + a_c] = kht[:, a_c * T:(a_c + 1) * T].astype(BF16)
            v_s[h, r0:r0 + rb, :] = jnp.where(lane_w == M_DEN_LANE[h], 1.0, vh).astype(BF16)
        sp = _split_bf16(_log_sigmoid(gates))
        cum_f = _sum_halves(_dot(tl_ref[...], sp))
        cum_b = _sum_halves(_dot(tu_ref[...], sp))
        g_s[r0:r0 + rb, :] = jnp.where(lane_g >= 3 * M_HEADS, cum_b, cum_f)
        lf = _log_sigmoid(gates_t)
        hi = lf.astype(BF16)
        lo = (lf - hi.astype(F32)).astype(BF16)
        both = jnp.concatenate([hi, lo], axis=0)
        row_f = _dot(both, tu_ref[...])
        row_b = _dot(both, tl_ref[...])
        rows_t = jnp.where((gate_id >= M_HEADS) & (gate_id < 2 * M_HEADS), row_f[:n_gates] + row_f[n_gates:],
                           jnp.where(gate_id >= 3 * M_HEADS, row_b[:n_gates] + row_b[n_gates:], gates_t))
        for a_c in range(rb // T):
            gr_s[r0 // T + a_c] = rows_t[:, a_c * T:(a_c + 1) * T]

    h_s[...] = jnp.zeros_like(h_s)
    c_s[...] = jnp.zeros_like(c_s)
    lane_h = lax.broadcasted_iota(jnp.int32, (T, M_WIN), 1)
    tt = lax.broadcasted_iota(jnp.int32, (T, T), 0)
    ss = lax.broadcasted_iota(jnp.int32, (T, T), 1)

    def step(n, ms):
        chains, where = [], []
        for d in range(2):
            fwd = d == 0
            c = n if fwd else n_chunks - 1 - n
            rows = pl.ds(pl.multiple_of(c * T, T), T)
            mask = (ss <= tt) if fwd else (ss >= tt)
            for h in range(M_HEADS):
                idx = d * M_HEADS + h
                chains.append((q_s[h, rows, :], kt_s[h, c], v_s[h, rows, :], mask, ms[idx], c_s.at[idx],
                               fwd, M_DEN_LANE[h]))
                where.append((h, rows, c))

        def gates_of():
            out = []
            for d in range(2):
                fwd = d == 0
                _, rows, c = where[d * M_HEADS]
                col = g_s[rows, :]
                row = gr_s[c]
                f0 = M_HEADS if fwd else 3 * M_HEADS
                i0 = 0 if fwd else 2 * M_HEADS
                for h in range(M_HEADS):
                    out.append((col[:, f0 + h:f0 + h + 1], row[f0 + h:f0 + h + 1, :],
                                row[i0 + h:i0 + h + 1, :]))
            return out

        new_ms = []
        for (h, rows, _), (hh, m_new) in zip(where, _mlstm_chunks(chains, gates_of)):
            valid = (lane_h >= M_WIN_OFF[h]) & (lane_h < M_WIN_OFF[h] + M_HEAD_DIM)
            h_s[h, rows, :] += jnp.where(valid, hh, 0.0)
            new_ms.append(m_new)
        return tuple(new_ms)

    lax.fori_loop(0, n_chunks, step, tuple(jnp.zeros((1, 1), F32) for _ in range(2 * M_HEADS)))

    for blk in range(s // rb):
        r0 = blk * rb
        for h in range(M_HEADS):
            a = M_WIN_START[h]
            hm = _sigmoid(om_ref[r0:r0 + rb, a:a + M_WIN].astype(F32)) * h_s[h, r0:r0 + rb, :]
            ms = jnp.sum(hm * hm, axis=-1, keepdims=True) * (1.0 / M_HEAD_DIM)
            y = (hm * lax.rsqrt(ms + EPS)) * ng_ref[h] + skip_ref[h] * xc_s[r0:r0 + rb, a:a + M_WIN]
            y = y * _silu(zm_ref[r0:r0 + rb, a:a + M_WIN].astype(F32))
            o_ref[r0:r0 + rb, h * M_WIN:(h + 1) * M_WIN] = y.astype(o_ref.dtype)


def _mlstm(proj, *params):
    b, s, _ = proj.shape
    rb = min(M_ROW_BLOCK, s)
    t = np.arange(rb)
    tri = ((t[:, None] // M_CHUNK == t[None, :] // M_CHUNK) & (t[None, :] <= t[:, None])).astype(np.float32)
    seq = lambda c: pl.BlockSpec((None, s, M_WIDTH), lambda bi, c=c: (bi, 0, c), pipeline_mode=pl.Buffered(1))
    const = lambda shape: pl.BlockSpec(shape, lambda bi: (0,) * len(shape))
    args = params + (jnp.asarray(tri, BF16), jnp.asarray(tri.T, BF16))
    return pl.pallas_call(
        _mlstm_body,
        grid=(b,),
        in_specs=[seq(0), seq(1), seq(2)] + [const(a.shape) for a in args],
        out_specs=pl.BlockSpec((None, s, M_HEADS * M_WIN), lambda bi: (bi, 0, 0), pipeline_mode=pl.Buffered(1)),
        out_shape=jax.ShapeDtypeStruct((b, s, M_HEADS * M_WIN), BF16),
        scratch_shapes=[
            pltpu.VMEM((s, M_WIDTH), F32),
            pltpu.VMEM((M_HEADS, s, M_WIN), BF16),
            pltpu.VMEM((M_HEADS, s // M_CHUNK, M_WIN, M_CHUNK), BF16),
            pltpu.VMEM((M_HEADS, s, M_WIN), BF16),
            pltpu.VMEM((s, LANES), F32),
            pltpu.VMEM((s // M_CHUNK, 4 * M_HEADS, M_CHUNK), F32),
            pltpu.VMEM((M_HEADS, s, M_WIN), F32),
            pltpu.VMEM((2 * M_HEADS, M_WIN, M_WIN), F32),
        ],
        compiler_params=pltpu.CompilerParams(
            dimension_semantics=("parallel",), vmem_limit_bytes=VMEM_LIMIT_BYTES),
        name="mlstm",
    )(proj, proj, proj, *args)


def _mlstm_params(m_conv_w, m_conv_b, m_wq, m_wk, m_wv, m_w_gates, m_b_gates, m_skip, m_norm_g, w_out_m):
    def dense(w):
        eye = jnp.eye(M_WIDTH // M_QKV_BLOCK, dtype=F32)
        return jnp.einsum("gh,gio->giho", eye, w).reshape(M_WIDTH, M_WIDTH)

    def head_pad(h, at_offset):
        lo = M_WIN_OFF[h] if at_offset else 0
        return (lo, M_WIN - M_HEAD_DIM - lo)

    def head_weights(w, at_offset):
        d = dense(w)
        out = []
        for h in range(M_HEADS):
            blk = d[M_WIN_START[h]:M_WIN_START[h] + M_WIN, h * M_HEAD_DIM:(h + 1) * M_HEAD_DIM]
            out.append(jnp.pad(blk, ((0, 0), head_pad(h, at_offset))))
        return jnp.stack(out).astype(BF16)

    def window(vec):
        out = []
        for h in range(M_HEADS):
            head = vec[h * M_HEAD_DIM:(h + 1) * M_HEAD_DIM]
            out.append(jnp.pad(head, (M_WIN_OFF[h], M_WIN - M_HEAD_DIM - M_WIN_OFF[h])))
        return jnp.stack(out)[:, None, :]

    n_gates = 4 * M_HEADS
    gw4 = m_w_gates.reshape(3, M_HEADS, M_HEAD_DIM, n_gates)
    gw = jnp.stack([
        jnp.stack([jnp.pad(gw4[x, h], (head_pad(h, x == 2), (0, 0))) for h in range(M_HEADS)])
        for x in range(3)])
    gwt = jnp.swapaxes(gw, -1, -2).astype(BF16)
    gw = jnp.pad(gw, ((0, 0), (0, 0), (0, 0), (0, LANES - n_gates))).astype(BF16)
    gb = jnp.pad(m_b_gates, (0, LANES - n_gates))[None, :]
    wo = []
    for h in range(M_HEADS):
        rows = w_out_m[h * M_HEAD_DIM:(h + 1) * M_HEAD_DIM]
        wo.append(jnp.pad(rows, (head_pad(h, True), (0, 0))))
    wk = head_weights(m_wk, False)
    return (m_conv_w, m_conv_b[None, :], head_weights(m_wq, False), wk, jnp.swapaxes(wk, 1, 2),
            head_weights(m_wv, True), gw, gwt, gb, m_b_gates[:, None], window(m_skip),
            window(m_norm_g)), jnp.concatenate(wo, axis=0).astype(BF16)


def _rope_tables(positions):
    half = ROPE_DIM // 2
    inv = ROPE_THETA ** (-jnp.arange(half, dtype=F32) / half)
    ang = positions.astype(F32)[..., None] * inv
    cos, sin = jnp.cos(ang), jnp.sin(ang)
    shp = positions.shape
    one = jnp.ones(shp + (A_QKDIM - ROPE_DIM,), F32)
    zero = lambda n: jnp.zeros(shp + (n,), F32)
    c = jnp.concatenate([cos, cos, one], axis=-1)
    sa = jnp.concatenate([-sin, zero(A_QKDIM - half)], axis=-1)
    sb = jnp.concatenate([zero(half), sin, zero(A_QKDIM - ROPE_DIM)], axis=-1)
    two = lambda t: jnp.concatenate([t, t], axis=-1)
    return two(c), two(sa), two(sb)


def kernel(x, positions, norm_g, w_in, m_conv_w, m_conv_b, m_wq, m_wk, m_wv, m_w_gates, m_b_gates,
           m_skip, m_norm_g, h_lb_logits, h_norm_g, a_lambda, a_norm_g, w_out, final_g):
    b, s, _ = x.shape
    rope_c, rope_sa, rope_sb = _rope_tables(positions)
    w_in_b = w_in.astype(BF16)
    w_out_b = w_out.astype(BF16)
    x2d = x.reshape(b * s, D_MODEL)
    for l in range(DEPTH):
        proj = _inproj(x2d, norm_g[l][None, :], w_in_b[l]).reshape(b, s, IN_COLS)
        m_args, w_out_m = _mlstm_params(m_conv_w[l], m_conv_b[l], m_wq[l], m_wk[l], m_wv[l], m_w_gates[l],
                                        m_b_gates[l], m_skip[l], m_norm_g[l], w_out[l, :M_WIDTH])
        y_m = _mlstm(proj, *m_args)
        y_h = _hgrn(proj, h_lb_logits, h_norm_g[l][None, :], l)
        lam_init = 0.8 - 0.6 * math.exp(-0.3 * l)
        y_a = _attention(proj, rope_c, rope_sa, rope_sb, a_lambda[l], a_norm_g[l][None, :], lam_init)
        x2d = _outproj(x2d, y_m.reshape(b * s, -1), y_h.reshape(b * s, -1), y_a.reshape(b * s, -1),
                       w_out_m, w_out_b[l, M_WIDTH:M_WIDTH + H_WIDTH], w_out_b[l, M_WIDTH + H_WIDTH:],
                       final_g[None, :], l == DEPTH - 1)
    return x2d.reshape(b, s, D_MODEL)
```

```python
import functools
import math

import numpy as np
import jax
import jax.numpy as jnp
from jax import lax
from jax.experimental import pallas as pl
from jax.experimental.pallas import tpu as pltpu

F32 = jnp.float32
BF16 = jnp.bfloat16

D_MODEL = 1024
DEPTH = 4
M_WIDTH = 768
M_HEADS = 4
M_HEAD_DIM = 192
M_QKV_BLOCK = 4
M_CONV = 5
H_WIDTH = 768
H_HEAD_DIM = 128
H_HEADS = 6
A_WIDTH = 512
A_HEADS = 4
A_VDIM = 128
A_QKDIM = 64
ROPE_DIM = 16
ROPE_THETA = 500000.0
IN_COLS = 3 * M_WIDTH + 5 * H_WIDTH + 4 * A_WIDTH
EPS = 1e-6

LANES = 128
MXU_DIM = 256
VMEM_LIMIT_BYTES = 56 * 1024 * 1024

M_WIN = MXU_DIM
M_WIN_START = (0, 128, 384, 512)
M_WIN_OFF = (0, 64, 0, 64)
M_CHUNK = 128
M_ROW_BLOCK = 256
H_CHUNK = 128
H_LEVELS = 7
H_SUB = 64
H_SAFE_LOG_DECAY = -80.0
H_BLOCKS_PER_STEP = 4
M_DEN_LANE = (192, 0, 192, 0)
Q_BLOCK = 256


def _nt(a, b):
    return lax.dot_general(a, b, (((1,), (1,)), ((), ())), preferred_element_type=F32)


def _tn(a, b):
    return lax.dot_general(a, b, (((0,), (0,)), ((), ())), preferred_element_type=F32)


def _dot(a, b):
    return jnp.dot(a, b, preferred_element_type=F32)


def _sigmoid(x):
    return 1.0 / (1.0 + jnp.exp(-x))


def _gate_sigmoid(x):
    return 0.5 * jnp.tanh(0.5 * x) + 0.5


def _silu(x):
    return x * _gate_sigmoid(x)


def _split_bf16(x):
    hi = x.astype(BF16)
    lo = (x - hi.astype(F32)).astype(BF16)
    return jnp.concatenate([hi, lo], axis=1)


def _sum_halves(r):
    return r[:, :LANES] + r[:, LANES:]


def _inproj_body(x_ref, g_ref, w_ref, o_ref, h_ref):
    @pl.when(pl.program_id(1) == 0)
    def _():
        xf = x_ref[...]
        ms = jnp.mean(xf * xf, axis=-1, keepdims=True)
        h_ref[...] = ((xf * lax.rsqrt(ms + EPS)) * g_ref[...]).astype(BF16)

    o_ref[...] = _dot(h_ref[...], w_ref[...]).astype(o_ref.dtype)


def _inproj(x2d, g, w, layer):
    m = x2d.shape[0]
    bm = min(1024, m)
    bn = 2048
    return pl.pallas_call(
        _inproj_body,
        grid=(m // bm, IN_COLS // bn),
        in_specs=[
            pl.BlockSpec((bm, D_MODEL), lambda i, j: (i, 0)),
            pl.BlockSpec((None, 1, D_MODEL), lambda i, j: (layer, 0, 0)),
            pl.BlockSpec((None, D_MODEL, bn), lambda i, j: (layer, 0, j)),
        ],
        out_specs=pl.BlockSpec((bm, bn), lambda i, j: (i, j)),
        out_shape=jax.ShapeDtypeStruct((m, IN_COLS), BF16),
        scratch_shapes=[pltpu.VMEM((bm, D_MODEL), BF16)],
        compiler_params=pltpu.CompilerParams(
            dimension_semantics=("parallel", "arbitrary"), vmem_limit_bytes=VMEM_LIMIT_BYTES),
        name="inproj",
    )(x2d, g, w)


def _outproj_body(x_ref, ym_ref, yh_ref, ya_ref, wm_ref, wh_ref, wa_ref, fg_ref, o_ref, *, final):
    acc = x_ref[...] + _dot(ym_ref[...], wm_ref[...])
    acc = acc + _dot(yh_ref[...], wh_ref[...])
    acc = acc + _dot(ya_ref[...], wa_ref[...])
    if final:
        ms = jnp.mean(acc * acc, axis=-1, keepdims=True)
        acc = (acc * lax.rsqrt(ms + EPS)) * fg_ref[...]
    o_ref[...] = acc


def _outproj(x2d, ym, yh, ya, wm, w_out, fg, layer, final):
    m = x2d.shape[0]
    bm = min(1024, m)
    km, kh, ka = ym.shape[1], yh.shape[1], ya.shape[1]
    return pl.pallas_call(
        functools.partial(_outproj_body, final=final),
        grid=(m // bm,),
        in_specs=[
            pl.BlockSpec((bm, D_MODEL), lambda i: (i, 0)),
            pl.BlockSpec((bm, km), lambda i: (i, 0)),
            pl.BlockSpec((bm, kh), lambda i: (i, 0)),
            pl.BlockSpec((bm, ka), lambda i: (i, 0)),
            pl.BlockSpec((None, km, D_MODEL), lambda i: (layer, 0, 0)),
            pl.BlockSpec((None, kh, D_MODEL), lambda i: (layer, M_WIDTH // kh, 0)),
            pl.BlockSpec((None, ka, D_MODEL), lambda i: (layer, (M_WIDTH + H_WIDTH) // ka, 0)),
            pl.BlockSpec((1, D_MODEL), lambda i: (0, 0)),
        ],
        out_specs=pl.BlockSpec((bm, D_MODEL), lambda i: (i, 0)),
        out_shape=jax.ShapeDtypeStruct((m, D_MODEL), F32),
        input_output_aliases={0: 0},
        compiler_params=pltpu.CompilerParams(
            dimension_semantics=("parallel",), vmem_limit_bytes=VMEM_LIMIT_BYTES),
        name="outproj",
    )(x2d, ym, yh, ya, wm, w_out, w_out, fg)


def _rope(x, c, sa, sb):
    return x * c + pltpu.roll(x, LANES - ROPE_DIM // 2, 1) * sa + pltpu.roll(x, ROPE_DIM // 2, 1) * sb


def _attn_body(q_ref, k_ref, v_ref, z_ref, c_ref, sa_ref, sb_ref, lam_ref, ng_ref, o_ref, kr_ref, vt_ref,
               *, lam_init):
    i = pl.program_id(1)
    bq = q_ref.shape[0]
    s = k_ref.shape[0]

    @pl.when(i == 0)
    def _():
        c, sa, sb = c_ref[...], sa_ref[...], sb_ref[...]
        for p in range(A_HEADS):
            sl = slice(p * LANES, (p + 1) * LANES)
            kr_ref[:, sl] = _rope(k_ref[:, sl].astype(F32), c, sa, sb).astype(BF16)
            for blk in range(s // bq):
                rows = slice(blk * bq, (blk + 1) * bq)
                vt_ref[p, :, rows] = v_ref[rows, sl].astype(F32).T.astype(BF16)

    lp = lam_ref[...]
    lam = (jnp.exp(jnp.sum(lp[0:1] * lp[1:2], axis=-1, keepdims=True))
           - jnp.exp(jnp.sum(lp[2:3] * lp[3:4], axis=-1, keepdims=True)) + lam_init)

    r0 = pl.multiple_of(i * bq, bq)
    c = c_ref[pl.ds(r0, bq), :]
    sa = sa_ref[pl.ds(r0, bq), :]
    sb = sb_ref[pl.ds(r0, bq), :]
    first = lax.broadcasted_iota(jnp.int32, (LANES, bq), 0) < A_QKDIM

    def scores(p):
        sl = slice(p * LANES, (p + 1) * LANES)
        qt = (_rope(q_ref[:, sl].astype(F32), c, sa, sb) * (A_QKDIM ** -0.5 * math.log2(math.e))).T
        ks = kr_ref[:, sl]
        return (_dot(ks, jnp.where(first, qt, 0.0).astype(BF16)),
                _dot(ks, jnp.where(first, 0.0, qt).astype(BF16)))

    nxt = scores(0)
    for p in range(A_HEADS):
        sl = slice(p * LANES, (p + 1) * LANES)
        s1, s2 = nxt
        if p + 1 < A_HEADS:
            nxt = scores(p + 1)
        e1 = jnp.exp2(s1 - jnp.max(s1, axis=0, keepdims=True))
        e2 = jnp.exp2(s2 - jnp.max(s2, axis=0, keepdims=True))
        n1 = 1.0 / jnp.sum(e1, axis=0, keepdims=True)
        n2 = lam / jnp.sum(e2, axis=0, keepdims=True)
        ot = _dot(vt_ref[p], e1.astype(BF16)) * n1 - _dot(vt_ref[p], e2.astype(BF16)) * n2
        o = ot.T
        ms = jnp.mean(o * o, axis=-1, keepdims=True)
        y = (o * lax.rsqrt(ms + EPS)) * ng_ref[:, sl] * (1.0 - lam_init)
        o_ref[:, sl] = (y * _silu(z_ref[:, sl].astype(F32))).astype(o_ref.dtype)


def _attention(proj, rope_c, rope_sa, rope_sb, lam_p, ng, layer, lam_init):
    b, s, _ = proj.shape
    bq = min(Q_BLOCK, s)
    col0 = (3 * M_WIDTH + 5 * H_WIDTH) // A_WIDTH
    seq_full = lambda c: pl.BlockSpec((None, s, A_WIDTH), lambda bi, i, c=c: (bi, 0, c))
    seq_blk = lambda c: pl.BlockSpec((None, bq, A_WIDTH), lambda bi, i, c=c: (bi, i, c))
    tab = pl.BlockSpec((None, s, LANES), lambda bi, i: (bi, 0, 0))
    return pl.pallas_call(
        functools.partial(_attn_body, lam_init=lam_init),
        grid=(b, s // bq),
        in_specs=[seq_blk(col0), seq_full(col0 + 1), seq_full(col0 + 2), seq_blk(col0 + 3),
                  tab, tab, tab,
                  pl.BlockSpec((None, 4, A_QKDIM), lambda bi, i: (layer, 0, 0)),
                  pl.BlockSpec((None, 1, A_WIDTH), lambda bi, i: (layer, 0, 0))],
        out_specs=pl.BlockSpec((None, bq, A_WIDTH), lambda bi, i: (bi, i, 0)),
        out_shape=jax.ShapeDtypeStruct((b, s, A_WIDTH), BF16),
        scratch_shapes=[pltpu.VMEM((s, A_WIDTH), BF16),
                        pltpu.VMEM((A_HEADS, A_VDIM, s), BF16)],
        compiler_params=pltpu.CompilerParams(
            dimension_semantics=("parallel", "arbitrary"), vmem_limit_bytes=VMEM_LIMIT_BYTES),
        name="diff_attn",
    )(proj, proj, proj, proj, rope_c, rope_sa, rope_sb, lam_p, ng)


def _hgrn_consts():
    t = np.arange(H_CHUNK)
    tt, r = t[:, None], t[None, :]
    blocks = []
    for lvl in range(H_LEVELS):
        start = (t >> lvl) << lvl
        end = start + (1 << lvl) - 1
        is_right = ((t >> lvl) & 1) == 1
        prefix = (r >= start[:, None]) & (r <= tt)
        suffix = (r > tt) & (r <= end[:, None])
        blocks.append(np.where(is_right[:, None], prefix, suffix))
    blocks.append(r <= tt)
    blocks.append(r > tt)
    w_f = np.stack(blocks).astype(np.float32)
    w_b = w_f[:, ::-1, ::-1]
    x = tt ^ r
    top = np.where(x > 0, np.floor(np.log2(np.maximum(x, 1))), -1).astype(np.int32)
    lvl_f = np.where(tt > r, top, -1).astype(np.int32)
    nblk = H_LEVELS + 2
    return (w_f.reshape(nblk * H_CHUNK, H_CHUNK), np.ascontiguousarray(w_b).reshape(nblk * H_CHUNK, H_CHUNK),
            lvl_f, np.ascontiguousarray(lvl_f.T))


def _hgrn_chunk(q, k, g, v, vt, w_ref, lvl, st_ref, fwd):
    T = H_CHUNK
    e = _sum_halves(_dot(w_ref[...], _split_bf16(g)))
    trow = lax.broadcasted_iota(jnp.int32, (T, 1), 0)
    scores = jnp.zeros((T, T), F32)
    for lv in range(H_LEVELS):
        q_role = ((trow >> lv) & 1) == (1 if fwd else 0)
        x = (jnp.where(q_role, q, k) * jnp.exp(e[lv * T:(lv + 1) * T])).astype(BF16)
        scores = jnp.where(lvl == lv, _nt(x, x), scores)
    diag = jnp.sum(q * k, axis=-1, keepdims=True)
    o = _dot(scores.astype(BF16), v) + diag * v.astype(F32)
    e_q = e[H_LEVELS * T:(H_LEVELS + 1) * T]
    e_k = e[(H_LEVELS + 1) * T:(H_LEVELS + 2) * T]
    st = st_ref[...]
    o = o + _nt((q * jnp.exp(e_q)).astype(BF16), st.astype(BF16))
    kt = (k * jnp.exp(e_k)).astype(BF16)
    e_tot = e_q[T - 1:T] if fwd else e_q[0:1]
    st_ref[...] = st * jnp.exp(e_tot) + _dot(vt, kt)
    return o


def _hgrn_fast_blocks(chains):
    T = H_CHUNK
    assert T == 2 * H_SUB
    in_a = lax.broadcasted_iota(jnp.int32, (T, 1), 0) < H_SUB
    cums = [_sum_halves(_dot(tri, _split_bf16(g))) for (_, _, g, _, _, tri, _, _) in chains]
    q_sub, q_blk, k_inv, k_sub, k_blk, decays = [], [], [], [], [], []
    for (q, k, _, _, _, _, _, fwd), cum in zip(chains, cums):
        e_a = jnp.exp(cum[H_SUB - 1:H_SUB] if fwd else cum[0:1])
        e_b = jnp.exp(cum[T - 1:T] if fwd else cum[H_SUB:H_SUB + 1])
        in_first = in_a if fwd else jnp.logical_not(in_a)
        e_first, e_second = (e_a, e_b) if fwd else (e_b, e_a)
        qs = q * jnp.exp(cum)
        ki = k * jnp.exp(-cum)
        ks = ki * jnp.where(in_first, e_first, e_second)
        q_sub.append(qs.astype(BF16))
        q_blk.append(jnp.where(in_first, qs, qs * e_first).astype(BF16))
        k_inv.append(ki.astype(BF16))
        k_sub.append(ks.astype(BF16))
        k_blk.append(jnp.where(in_first, ks * e_second, ks).astype(BF16))
        decays.append(e_first * e_second)
    same = [_nt(qs, ki) for qs, ki in zip(q_sub, k_inv)]
    cross = [_nt(qs, ks) for qs, ks in zip(q_sub, k_sub)]
    incs = [_dot(vt, kb) for (_, _, _, _, vt, _, _, _), kb in zip(chains, k_blk)]
    outs = [_dot(jnp.where(kind == 1, sm, jnp.where(kind == 2, cr, 0.0)).astype(BF16), v)
            for (_, _, _, v, _, _, kind, _), sm, cr in zip(chains, same, cross)]
    return outs, q_blk, incs, decays


def _hgrn_body(q_ref, ff_ref, fb_ref, i_ref, z_ref, lb_ref, ng_ref, wf_ref, wb_ref, lf_ref, lbm_ref,
               tf_ref, tb_ref, mf_ref, mb_ref, o_ref, k_s, g_s, acc_s, st_s, vt_s, qt_s, inc_s, dec_s,
               *, layer):
    s = q_ref.shape[0]
    n_chunks = s // H_CHUNK
    n_sub = s // H_SUB
    lowest = None
    for d, pre_ref in enumerate((ff_ref, fb_ref)):
        z = lb_ref[d]
        p = jnp.exp(z - jnp.max(z, axis=0, keepdims=True))
        p = p / jnp.sum(p, axis=0, keepdims=True)
        lb = jnp.sum(p[1:layer + 1], axis=0, keepdims=True) if layer > 0 else jnp.zeros((1, LANES), F32)
        f = lb + (1.0 - lb) * _sigmoid(pre_ref[...].astype(F32))
        g = jnp.log(f)
        g_s[d] = g
        k_s[d] = 1.0 - f
        tot = jnp.min(jnp.sum(g.reshape(n_sub, H_SUB, LANES), axis=1))
        lowest = tot if lowest is None else jnp.minimum(lowest, tot)
    for c in range(n_chunks):
        vt_s[c] = i_ref[c * H_CHUNK:(c + 1) * H_CHUNK, :].astype(F32).T.astype(BF16)

    def block_args(c, d):
        rows = pl.ds(pl.multiple_of(c * H_CHUNK, H_CHUNK), H_CHUNK)
        return rows, (q_ref[rows, :].astype(F32), k_s[d, rows, :], g_s[d, rows, :], i_ref[rows, :], vt_s[c])

    def fast():
        tris = (tf_ref[...], tb_ref[...])
        kinds = (mf_ref[...], mb_ref[...])
        per_step = math.gcd(H_BLOCKS_PER_STEP, n_chunks)

        def intra(i, carry):
            where, chains = [], []
            for u in range(per_step):
                c = i * per_step + u
                for d in range(2):
                    rows, args = block_args(c, d)
                    where.append((c, d, rows))
                    chains.append(args + (tris[d], kinds[d], d == 0))
            outs, q_blk, incs, decays = _hgrn_fast_blocks(chains)
            for (c, d, rows), o, qb, inc, dec in zip(where, outs, q_blk, incs, decays):
                if d == 0:
                    acc_s[rows, :] = o
                else:
                    acc_s[rows, :] += o
                qt_s[d, rows, :] = qb
                inc_s[d, c] = inc
                dec_s[d, c] = dec
            return carry

        lax.fori_loop(0, n_chunks // per_step, intra, 0)

        def scan(j, states):
            new = []
            for d in range(2):
                c = j if d == 0 else n_chunks - 1 - j
                rows = pl.ds(pl.multiple_of(c * H_CHUNK, H_CHUNK), H_CHUNK)
                acc_s[rows, :] += _nt(qt_s[d, rows, :], states[d].astype(BF16))
                new.append(states[d] * dec_s[d, c] + inc_s[d, c])
            return tuple(new)

        zero = jnp.zeros((H_HEAD_DIM, H_HEAD_DIM), F32)
        lax.fori_loop(0, n_chunks, scan, (zero, zero), unroll=4)

    def safe():
        acc_s[...] = jnp.zeros_like(acc_s)
        st_s[...] = jnp.zeros_like(st_s)
        lvl_f = lf_ref[...]
        lvl_b = lbm_ref[...]

        def step(n, carry):
            for d in range(2):
                rows, args = block_args(n if d == 0 else n_chunks - 1 - n, d)
                acc_s[rows, :] += _hgrn_chunk(*args, wf_ref if d == 0 else wb_ref,
                                              lvl_f if d == 0 else lvl_b, st_s.at[d], d == 0)
            return carry

        lax.fori_loop(0, n_chunks, step, 0)

    lax.cond(lowest >= H_SAFE_LOG_DECAY, fast, safe)
    o = acc_s[...]
    ms = jnp.mean(o * o, axis=-1, keepdims=True)
    y = (o * lax.rsqrt(ms + EPS)) * ng_ref[...]
    o_ref[...] = (y * _silu(z_ref[...].astype(F32))).astype(o_ref.dtype)


def _hgrn(proj, lb_logits, ng, layer):
    b, s, _ = proj.shape
    col0 = 3 * M_WIDTH // H_HEAD_DIM
    w_f, w_b, lvl_f, lvl_b = _hgrn_consts()
    tri_f, tri_b, kind_f, kind_b = _hgrn_fast_consts()
    seq = lambda g: pl.BlockSpec((None, s, H_HEAD_DIM), lambda bi, h, g=g: (bi, 0, col0 + g * H_HEADS + h))
    const = lambda shape: pl.BlockSpec(shape, lambda bi, h: (0,) * len(shape))
    consts = (jnp.asarray(w_f, BF16), jnp.asarray(w_b, BF16), jnp.asarray(lvl_f), jnp.asarray(lvl_b),
              jnp.asarray(tri_f, BF16), jnp.asarray(tri_b, BF16), jnp.asarray(kind_f), jnp.asarray(kind_b))
    n_chunks = s // H_CHUNK
    return pl.pallas_call(
        functools.partial(_hgrn_body, layer=layer),
        grid=(b, H_HEADS),
        in_specs=[seq(0), seq(1), seq(2), seq(3), seq(4),
                  pl.BlockSpec((2, DEPTH, H_HEAD_DIM), lambda bi, h: (0, 0, h)),
                  pl.BlockSpec((None, 1, H_HEAD_DIM), lambda bi, h: (layer, 0, h))]
                 + [const(c.shape) for c in consts],
        out_specs=pl.BlockSpec((None, s, H_HEAD_DIM), lambda bi, h: (bi, 0, h)),
        out_shape=jax.ShapeDtypeStruct((b, s, H_WIDTH), BF16),
        scratch_shapes=[pltpu.VMEM((2, s, H_HEAD_DIM), F32),
                        pltpu.VMEM((2, s, H_HEAD_DIM), F32),
                        pltpu.VMEM((s, H_HEAD_DIM), F32),
                        pltpu.VMEM((2, H_HEAD_DIM, H_HEAD_DIM), F32),
                        pltpu.VMEM((s // H_CHUNK, H_HEAD_DIM, H_CHUNK), BF16),
                        pltpu.VMEM((2, s, H_HEAD_DIM), BF16),
                        pltpu.VMEM((2, n_chunks, H_HEAD_DIM, H_HEAD_DIM), F32),
                        pltpu.VMEM((2, n_chunks, 1, H_HEAD_DIM), F32)],
        compiler_params=pltpu.CompilerParams(
            dimension_semantics=("parallel", "arbitrary"), vmem_limit_bytes=VMEM_LIMIT_BYTES),
        name="hgrn2",
    )(proj, proj, proj, proj, proj, lb_logits, ng, *consts)


def _hgrn_fast_consts():
    t = np.arange(H_CHUNK)
    tt, r = t[:, None], t[None, :]
    same = (tt // H_SUB) == (r // H_SUB)
    tri_f, tri_b = same & (r <= tt), same & (r >= tt)
    kind_f = np.where(tri_f, 1, np.where((tt >= H_SUB) & (r < H_SUB), 2, 0)).astype(np.int32)
    kind_b = np.where(tri_b, 1, np.where((tt < H_SUB) & (r >= H_SUB), 2, 0)).astype(np.int32)
    return tri_f.astype(np.float32), tri_b.astype(np.float32), kind_f, kind_b


def _log_sigmoid(x):
    return jnp.minimum(x, 0.0) - jnp.log(1.0 + jnp.exp(-jnp.abs(x)))


def _mlstm_chunks(chains, gates_of):
    T = M_CHUNK
    cexts = [c_ref[...] for (_, _, _, _, _, c_ref, _, _) in chains]
    sc = [_dot(q, kt) for (q, kt, _, _, _, _, _, _) in chains]
    qc = [_dot(q, cext.astype(BF16)) for (q, _, _, _, _, _, _, _), cext in zip(chains, cexts)]
    gates = gates_of()
    upd, state = [], []
    for (_, kt, vext, _, m_prev, _, fwd, _), (bc, br, ir) in zip(chains, gates):
        b_last = bc[T - 1:T] if fwd else bc[0:1]
        dec = b_last - (br - ir)
        m_new = jnp.maximum(b_last + m_prev, jnp.max(dec, axis=-1, keepdims=True))
        upd.append(_dot((kt.astype(F32) * jnp.exp(dec - m_new)).astype(BF16), vext))
        state.append((jnp.exp(b_last + m_prev - m_new), m_new))
    ws, scale = [], []
    for (_, _, _, mask, m_prev, _, _, _), (bc, br, ir), s_qk in zip(chains, gates, sc):
        a = jnp.where(mask, ir - br, -jnp.inf)
        mu = jnp.maximum(jnp.max(a, axis=-1, keepdims=True), m_prev)
        ws.append((jnp.exp(a - mu) * s_qk).astype(BF16))
        scale.append((jnp.exp(m_prev - mu), jnp.exp(-(bc + mu))))
    out = []
    for chain, cext, w, q_c, u, (g_inter, floor), (decay, m_new) in zip(chains, cexts, ws, qc, upd, scale, state):
        vext, c_ref, den_lane = chain[2], chain[5], chain[7]
        nd = _dot(w, vext) + g_inter * q_c
        den = nd[:, den_lane:den_lane + 1]
        c_ref[...] = decay * cext + u
        out.append((nd, jnp.maximum(jnp.abs(den), floor), m_new))
    return out


def _mlstm_body(xm_ref, om_ref, zm_ref, cw_ref, cb_ref, wq_ref, wk_ref, wkt_ref, wv_ref, gw_ref, gwt_ref,
                gb_ref, gbt_ref, skip_ref, ng_ref, tl_ref, tu_ref, o_ref,
                xc_s, q_s, kt_s, v_s, g_s, gr_s, nd_s, dd_s, c_s):
    s = xm_ref.shape[0]
    T = M_CHUNK
    n_chunks = s // T
    rb = min(M_ROW_BLOCK, s)
    halo = 16
    n_gates = 4 * M_HEADS

    lane_w = lax.broadcasted_iota(jnp.int32, (rb, M_WIN), 1)
    lane_g = lax.broadcasted_iota(jnp.int32, (rb, LANES), 1)
    gate_id = lax.broadcasted_iota(jnp.int32, (n_gates, rb), 0)
    for blk in range(s // rb):
        r0 = blk * rb
        mid = xm_ref[r0:r0 + rb, :].astype(F32)
        top = xm_ref[r0 - halo:r0, :].astype(F32) if blk > 0 else jnp.zeros((halo, M_WIDTH), F32)
        bot = (xm_ref[r0 + rb:r0 + rb + halo, :].astype(F32) if r0 + rb < s
               else jnp.zeros((halo, M_WIDTH), F32))
        xp = jnp.concatenate([top, mid, bot], axis=0)
        n_rows = rb + 2 * halo
        conv = jnp.zeros((rb, M_WIDTH), F32) + cb_ref[...]
        for j in range(M_CONV):
            shift = (M_CONV // 2 - j) % n_rows
            sh = xp if shift == 0 else pltpu.roll(xp, shift, 0)
            conv = conv + sh[halo:halo + rb] * cw_ref[j:j + 1, :]
        xc = _silu(conv)
        xc_s[r0:r0 + rb, :] = xc
        xcb = xc.astype(BF16)
        gates = jnp.zeros((rb, LANES), F32) + gb_ref[...]
        gates_t = jnp.zeros((n_gates, rb), F32) + gbt_ref[...]
        for h in range(M_HEADS):
            a = M_WIN_START[h]
            xw = xcb[:, a:a + M_WIN]
            qh = _dot(xw, wq_ref[h])
            kh = _dot(xw, wk_ref[h])
            kht = _nt(wkt_ref[h], xw)
            vh = _dot(xm_ref[r0:r0 + rb, a:a + M_WIN], wv_ref[h])
            qb, kb, vb = qh.astype(BF16), kh.astype(BF16), vh.astype(BF16)
            gates = gates + _dot(qb, gw_ref[0, h]) + _dot(kb, gw_ref[1, h]) + _dot(vb, gw_ref[2, h])
            gates_t = gates_t + _nt(gwt_ref[0, h], qb) + _nt(gwt_ref[1, h], kb) + _nt(gwt_ref[2, h], vb)
            q_s[h, r0:r0 + rb, :] = (qh * (M_HEAD_DIM ** -0.5)).astype(BF16)
            for a_c in range(rb // T):
                kt_s[h, r0 // T + a_c] = kht[:, a_c * T:(a_c + 1) * T].astype(BF16)
            v_s[h, r0:r0 + rb, :] = jnp.where(lane_w == M_DEN_LANE[h], 1.0, vh).astype(BF16)
        sp = _split_bf16(_log_sigmoid(gates))
        cum_f = _sum_halves(_dot(tl_ref[...], sp))
        cum_b = _sum_halves(_dot(tu_ref[...], sp))
        g_s[r0:r0 + rb, :] = jnp.where(lane_g >= 3 * M_HEADS, cum_b, cum_f)
        lf = _log_sigmoid(gates_t)
        hi = lf.astype(BF16)
        lo = (lf - hi.astype(F32)).astype(BF16)
        both = jnp.concatenate([hi, lo], axis=0)
        row_f = _dot(both, tu_ref[...])
        row_b = _dot(both, tl_ref[...])
        rows_t = jnp.where((gate_id >= M_HEADS) & (gate_id < 2 * M_HEADS), row_f[:n_gates] + row_f[n_gates:],
                           jnp.where(gate_id >= 3 * M_HEADS, row_b[:n_gates] + row_b[n_gates:], gates_t))
        for a_c in range(rb // T):
            gr_s[r0 // T + a_c] = rows_t[:, a_c * T:(a_c + 1) * T]

    c_s[...] = jnp.zeros_like(c_s)
    tt = lax.broadcasted_iota(jnp.int32, (T, T), 0)
    ss = lax.broadcasted_iota(jnp.int32, (T, T), 1)

    def step(n, ms):
        chains, where = [], []
        for d in range(2):
            fwd = d == 0
            c = n if fwd else n_chunks - 1 - n
            rows = pl.ds(pl.multiple_of(c * T, T), T)
            mask = (ss <= tt) if fwd else (ss >= tt)
            for h in range(M_HEADS):
                idx = d * M_HEADS + h
                chains.append((q_s[h, rows, :], kt_s[h, c], v_s[h, rows, :], mask, ms[idx], c_s.at[idx],
                               fwd, M_DEN_LANE[h]))
                where.append((h, rows, c))

        def gates_of():
            out = []
            for d in range(2):
                fwd = d == 0
                _, rows, c = where[d * M_HEADS]
                col = g_s[rows, :]
                row = gr_s[c]
                f0 = M_HEADS if fwd else 3 * M_HEADS
                i0 = 0 if fwd else 2 * M_HEADS
                for h in range(M_HEADS):
                    out.append((col[:, f0 + h:f0 + h + 1], row[f0 + h:f0 + h + 1, :],
                                row[i0 + h:i0 + h + 1, :]))
            return out

        new_ms = []
        for idx, ((h, rows, _), (nd, dd, m_new)) in enumerate(zip(where, _mlstm_chunks(chains, gates_of))):
            nd_s[idx // M_HEADS, h, rows, :] = nd.astype(BF16)
            dd_s[rows, idx:idx + 1] = dd
            new_ms.append(m_new)
        return tuple(new_ms)

    lax.fori_loop(0, n_chunks, step, tuple(jnp.zeros((1, 1), F32) for _ in range(2 * M_HEADS)))

    for blk in range(s // rb):
        r0 = blk * rb
        inv_dd = 1.0 / dd_s[r0:r0 + rb, 0:2 * M_HEADS]
        for h in range(M_HEADS):
            a = M_WIN_START[h]
            valid = (lane_w >= M_WIN_OFF[h]) & (lane_w < M_WIN_OFF[h] + M_HEAD_DIM)
            hsum = (nd_s[0, h, r0:r0 + rb, :].astype(F32) * inv_dd[:, h:h + 1]
                    + nd_s[1, h, r0:r0 + rb, :].astype(F32) * inv_dd[:, M_HEADS + h:M_HEADS + h + 1])
            hm = _gate_sigmoid(om_ref[r0:r0 + rb, a:a + M_WIN].astype(F32)) * jnp.where(valid, hsum, 0.0)
            ms = jnp.sum(hm * hm, axis=-1, keepdims=True) * (1.0 / M_HEAD_DIM)
            y = (hm * lax.rsqrt(ms + EPS)) * ng_ref[h] + skip_ref[h] * xc_s[r0:r0 + rb, a:a + M_WIN]
            y = y * _silu(zm_ref[r0:r0 + rb, a:a + M_WIN].astype(F32))
            o_ref[r0:r0 + rb, h * M_WIN:(h + 1) * M_WIN] = y.astype(o_ref.dtype)


def _mlstm(proj, params, layer):
    b, s, _ = proj.shape
    rb = min(M_ROW_BLOCK, s)
    t = np.arange(rb)
    tri = ((t[:, None] // M_CHUNK == t[None, :] // M_CHUNK) & (t[None, :] <= t[:, None])).astype(np.float32)
    seq = lambda c: pl.BlockSpec((None, s, M_WIDTH), lambda bi, c=c: (bi, 0, c), pipeline_mode=pl.Buffered(1))
    const = lambda shape: pl.BlockSpec(shape, lambda bi: (0,) * len(shape))
    of_layer = lambda a: pl.BlockSpec((None,) + a.shape[1:], lambda bi: (layer,) + (0,) * (a.ndim - 1))
    tris = (jnp.asarray(tri, BF16), jnp.asarray(tri.T, BF16))
    return pl.pallas_call(
        _mlstm_body,
        grid=(b,),
        in_specs=[seq(0), seq(1), seq(2)] + [of_layer(a) for a in params] + [const(t.shape) for t in tris],
        out_specs=pl.BlockSpec((None, s, M_HEADS * M_WIN), lambda bi: (bi, 0, 0), pipeline_mode=pl.Buffered(1)),
        out_shape=jax.ShapeDtypeStruct((b, s, M_HEADS * M_WIN), BF16),
        scratch_shapes=[
            pltpu.VMEM((s, M_WIDTH), F32),
            pltpu.VMEM((M_HEADS, s, M_WIN), BF16),
            pltpu.VMEM((M_HEADS, s // M_CHUNK, M_WIN, M_CHUNK), BF16),
            pltpu.VMEM((M_HEADS, s, M_WIN), BF16),
            pltpu.VMEM((s, LANES), F32),
            pltpu.VMEM((s // M_CHUNK, 4 * M_HEADS, M_CHUNK), F32),
            pltpu.VMEM((2, M_HEADS, s, M_WIN), BF16),
            pltpu.VMEM((s, LANES), F32),
            pltpu.VMEM((2 * M_HEADS, M_WIN, M_WIN), F32),
        ],
        compiler_params=pltpu.CompilerParams(
            dimension_semantics=("parallel",), vmem_limit_bytes=VMEM_LIMIT_BYTES),
        name="mlstm",
    )(proj, proj, proj, *params, *tris)


def _mlstm_params(m_conv_w, m_conv_b, m_wq, m_wk, m_wv, m_w_gates, m_b_gates, m_skip, m_norm_g, w_out_m):
    def dense(w):
        eye = jnp.eye(M_WIDTH // M_QKV_BLOCK, dtype=F32)
        return jnp.einsum("gh,gio->giho", eye, w).reshape(M_WIDTH, M_WIDTH)

    def head_pad(h, at_offset):
        lo = M_WIN_OFF[h] if at_offset else 0
        return (lo, M_WIN - M_HEAD_DIM - lo)

    def head_weights(w, at_offset):
        d = dense(w)
        out = []
        for h in range(M_HEADS):
            blk = d[M_WIN_START[h]:M_WIN_START[h] + M_WIN, h * M_HEAD_DIM:(h + 1) * M_HEAD_DIM]
            out.append(jnp.pad(blk, ((0, 0), head_pad(h, at_offset))))
        return jnp.stack(out).astype(BF16)

    def window(vec):
        out = []
        for h in range(M_HEADS):
            head = vec[h * M_HEAD_DIM:(h + 1) * M_HEAD_DIM]
            out.append(jnp.pad(head, (M_WIN_OFF[h], M_WIN - M_HEAD_DIM - M_WIN_OFF[h])))
        return jnp.stack(out)[:, None, :]

    n_gates = 4 * M_HEADS
    gw4 = m_w_gates.reshape(3, M_HEADS, M_HEAD_DIM, n_gates)
    gw = jnp.stack([
        jnp.stack([jnp.pad(gw4[x, h], (head_pad(h, x == 2), (0, 0))) for h in range(M_HEADS)])
        for x in range(3)])
    gwt = jnp.swapaxes(gw, -1, -2).astype(BF16)
    gw = jnp.pad(gw, ((0, 0), (0, 0), (0, 0), (0, LANES - n_gates))).astype(BF16)
    gb = jnp.pad(m_b_gates, (0, LANES - n_gates))[None, :]
    wo = []
    for h in range(M_HEADS):
        rows = w_out_m[h * M_HEAD_DIM:(h + 1) * M_HEAD_DIM]
        wo.append(jnp.pad(rows, (head_pad(h, True), (0, 0))))
    wk = head_weights(m_wk, False)
    return (m_conv_w, m_conv_b[None, :], head_weights(m_wq, False), wk, jnp.swapaxes(wk, 1, 2),
            head_weights(m_wv, True), gw, gwt, gb, m_b_gates[:, None], window(m_skip),
            window(m_norm_g)), jnp.concatenate(wo, axis=0).astype(BF16)


def _rope_tables(positions):
    half = ROPE_DIM // 2
    d = jnp.arange(LANES) % A_QKDIM
    inv = jnp.where(d < ROPE_DIM, ROPE_THETA ** (-(d % half).astype(F32) / half), 0.0)
    ang = positions.astype(F32)[..., None] * inv
    cos, sin = jnp.cos(ang), jnp.sin(ang)
    return (jnp.where(d < ROPE_DIM, cos, 1.0), jnp.where(d < half, -sin, 0.0),
            jnp.where((d >= half) & (d < ROPE_DIM), sin, 0.0))


def kernel(x, positions, norm_g, w_in, m_conv_w, m_conv_b, m_wq, m_wk, m_wv, m_w_gates, m_b_gates,
           m_skip, m_norm_g, h_lb_logits, h_norm_g, a_lambda, a_norm_g, w_out, final_g):
    b, s, _ = x.shape
    rope_c, rope_sa, rope_sb = _rope_tables(positions)
    w_in_b = w_in.astype(BF16)
    w_out_b = w_out.astype(BF16)
    m_params, w_out_m = jax.vmap(_mlstm_params)(m_conv_w, m_conv_b, m_wq, m_wk, m_wv, m_w_gates, m_b_gates,
                                                m_skip, m_norm_g, w_out[:, :M_WIDTH])
    lb_logits = jnp.swapaxes(h_lb_logits, 0, 1)
    x2d = x.reshape(b * s, D_MODEL)
    for l in range(DEPTH):
        proj = _inproj(x2d, norm_g[:, None, :], w_in_b, l).reshape(b, s, IN_COLS)
        y_m = _mlstm(proj, m_params, l)
        y_h = _hgrn(proj, lb_logits, h_norm_g[:, None, :], l)
        lam_init = 0.8 - 0.6 * math.exp(-0.3 * l)
        y_a = _attention(proj, rope_c, rope_sa, rope_sb, a_lambda, a_norm_g[:, None, :], l, lam_init)
        x2d = _outproj(x2d, y_m.reshape(b * s, -1), y_h.reshape(b * s, -1), y_a.reshape(b * s, -1),
                       w_out_m, w_out_b, final_g[None, :], l, l == DEPTH - 1)
    return x2d.reshape(b, s, D_MODEL)
```

```python
import functools
import math

import numpy as np
import jax
import jax.numpy as jnp
from jax import lax
from jax.experimental import pallas as pl
from jax.experimental.pallas import tpu as pltpu

F32 = jnp.float32
BF16 = jnp.bfloat16

D_MODEL = 1024
DEPTH = 4
M_WIDTH = 768
M_HEADS = 4
M_HEAD_DIM = 192
M_QKV_BLOCK = 4
M_CONV = 5
H_WIDTH = 768
H_HEAD_DIM = 128
H_HEADS = 6
A_WIDTH = 512
A_HEADS = 4
A_VDIM = 128
A_QKDIM = 64
ROPE_DIM = 16
ROPE_THETA = 500000.0
IN_COLS = 3 * M_WIDTH + 5 * H_WIDTH + 4 * A_WIDTH
EPS = 1e-6

LANES = 128
MXU_DIM = 256
VMEM_LIMIT_BYTES = 56 * 1024 * 1024
M_VMEM_LIMIT_BYTES = 60 * 1024 * 1024

M_WIN = MXU_DIM
M_WIN_START = (0, 128, 384, 512)
M_WIN_OFF = (0, 64, 0, 64)
M_CHUNK = 128
M_ROW_BLOCK = 256
H_CHUNK = 128
H_LEVELS = 7
H_SUB = 64
H_SAFE_LOG_DECAY = -80.0
H_BLOCKS_PER_STEP = 4
M_DEN_LANE = (192, 0, 192, 0)
Q_BLOCK = 256


def _nt(a, b):
    return lax.dot_general(a, b, (((1,), (1,)), ((), ())), preferred_element_type=F32)


def _tn(a, b):
    return lax.dot_general(a, b, (((0,), (0,)), ((), ())), preferred_element_type=F32)


def _dot(a, b):
    return jnp.dot(a, b, preferred_element_type=F32)


def _sigmoid(x):
    return 1.0 / (1.0 + jnp.exp(-x))


def _gate_sigmoid(x):
    return 0.5 * jnp.tanh(0.5 * x) + 0.5


def _silu(x):
    return x * _gate_sigmoid(x)


def _split_bf16(x):
    hi = x.astype(BF16)
    lo = (x - hi.astype(F32)).astype(BF16)
    return jnp.concatenate([hi, lo], axis=1)


def _sum_halves(r):
    return r[:, :LANES] + r[:, LANES:]


def _inproj_body(x_ref, g_ref, w_ref, o_ref, h_ref):
    @pl.when(pl.program_id(1) == 0)
    def _():
        xf = x_ref[...]
        ms = jnp.mean(xf * xf, axis=-1, keepdims=True)
        h_ref[...] = ((xf * lax.rsqrt(ms + EPS)) * g_ref[...]).astype(BF16)

    o_ref[...] = _dot(h_ref[...], w_ref[...]).astype(o_ref.dtype)


def _inproj(x2d, g, w, layer):
    m = x2d.shape[0]
    bm = min(1024, m)
    bn = 4096
    return pl.pallas_call(
        _inproj_body,
        grid=(m // bm, IN_COLS // bn),
        in_specs=[
            pl.BlockSpec((bm, D_MODEL), lambda i, j: (i, 0)),
            pl.BlockSpec((None, 1, D_MODEL), lambda i, j: (layer, 0, 0)),
            pl.BlockSpec((None, D_MODEL, bn), lambda i, j: (layer, 0, j)),
        ],
        out_specs=pl.BlockSpec((bm, bn), lambda i, j: (i, j)),
        out_shape=jax.ShapeDtypeStruct((m, IN_COLS), BF16),
        scratch_shapes=[pltpu.VMEM((bm, D_MODEL), BF16)],
        compiler_params=pltpu.CompilerParams(
            dimension_semantics=("parallel", "arbitrary"), vmem_limit_bytes=VMEM_LIMIT_BYTES),
        name="inproj",
    )(x2d, g, w)


def _outproj_body(x_ref, ym_ref, yh_ref, ya_ref, wm_ref, wh_ref, wa_ref, fg_ref, o_ref, *, final):
    acc = x_ref[...] + _dot(ym_ref[...], wm_ref[...])
    acc = acc + _dot(yh_ref[...], wh_ref[...])
    acc = acc + _dot(ya_ref[...], wa_ref[...])
    if final:
        ms = jnp.mean(acc * acc, axis=-1, keepdims=True)
        acc = (acc * lax.rsqrt(ms + EPS)) * fg_ref[...]
    o_ref[...] = acc


def _outproj(x2d, ym, yh, ya, wm, w_out, fg, layer, final):
    m = x2d.shape[0]
    bm = min(1024, m)
    km, kh, ka = ym.shape[1], yh.shape[1], ya.shape[1]
    return pl.pallas_call(
        functools.partial(_outproj_body, final=final),
        grid=(m // bm,),
        in_specs=[
            pl.BlockSpec((bm, D_MODEL), lambda i: (i, 0)),
            pl.BlockSpec((bm, km), lambda i: (i, 0)),
            pl.BlockSpec((bm, kh), lambda i: (i, 0)),
            pl.BlockSpec((bm, ka), lambda i: (i, 0)),
            pl.BlockSpec((None, km, D_MODEL), lambda i: (layer, 0, 0)),
            pl.BlockSpec((None, kh, D_MODEL), lambda i: (layer, M_WIDTH // kh, 0)),
            pl.BlockSpec((None, ka, D_MODEL), lambda i: (layer, (M_WIDTH + H_WIDTH) // ka, 0)),
            pl.BlockSpec((1, D_MODEL), lambda i: (0, 0)),
        ],
        out_specs=pl.BlockSpec((bm, D_MODEL), lambda i: (i, 0)),
        out_shape=jax.ShapeDtypeStruct((m, D_MODEL), F32),
        input_output_aliases={0: 0} if layer > 0 else {},
        compiler_params=pltpu.CompilerParams(
            dimension_semantics=("parallel",), vmem_limit_bytes=VMEM_LIMIT_BYTES),
        name="outproj",
    )(x2d, ym, yh, ya, wm, w_out, w_out, fg)


def _rope(x, c, sa, sb):
    return x * c + pltpu.roll(x, LANES - ROPE_DIM // 2, 1) * sa + pltpu.roll(x, ROPE_DIM // 2, 1) * sb


def _attn_body(q_ref, k_ref, v_ref, z_ref, ct_ref, sat_ref, sbt_ref, lam_ref, ng_ref, o_ref, kr_ref, vt_ref,
               c_ref, sa_ref, sb_ref, *, lam_init):
    i = pl.program_id(1)
    bq = q_ref.shape[0]
    s = k_ref.shape[0]

    @pl.when(i == 0)
    def _():
        for src, dst in ((ct_ref, c_ref), (sat_ref, sa_ref), (sbt_ref, sb_ref)):
            for blk in range(s // LANES):
                rows = slice(blk * LANES, (blk + 1) * LANES)
                dst[rows, :] = src[:, rows].T
        c, sa, sb = c_ref[...], sa_ref[...], sb_ref[...]
        for p in range(A_HEADS):
            sl = slice(p * LANES, (p + 1) * LANES)
            kr_ref[:, sl] = _rope(k_ref[:, sl].astype(F32), c, sa, sb).astype(BF16)
            for blk in range(s // bq):
                rows = slice(blk * bq, (blk + 1) * bq)
                vt_ref[p, :, rows] = v_ref[rows, sl].astype(F32).T.astype(BF16)

    lp = lam_ref[...]
    lam = (jnp.exp(jnp.sum(lp[0:1] * lp[1:2], axis=-1, keepdims=True))
           - jnp.exp(jnp.sum(lp[2:3] * lp[3:4], axis=-1, keepdims=True)) + lam_init)

    r0 = pl.multiple_of(i * bq, bq)
    c = c_ref[pl.ds(r0, bq), :]
    sa = sa_ref[pl.ds(r0, bq), :]
    sb = sb_ref[pl.ds(r0, bq), :]
    first = lax.broadcasted_iota(jnp.int32, (LANES, bq), 0) < A_QKDIM

    def scores(p):
        sl = slice(p * LANES, (p + 1) * LANES)
        qt = (_rope(q_ref[:, sl].astype(F32), c, sa, sb) * (A_QKDIM ** -0.5 * math.log2(math.e))).T
        ks = kr_ref[:, sl]
        return (_dot(ks, jnp.where(first, qt, 0.0).astype(BF16)),
                _dot(ks, jnp.where(first, 0.0, qt).astype(BF16)))

    nxt = scores(0)
    for p in range(A_HEADS):
        sl = slice(p * LANES, (p + 1) * LANES)
        s1, s2 = nxt
        if p + 1 < A_HEADS:
            nxt = scores(p + 1)
        e1 = jnp.exp2(s1 - jnp.max(s1, axis=0, keepdims=True))
        e2 = jnp.exp2(s2 - jnp.max(s2, axis=0, keepdims=True))
        n1 = 1.0 / jnp.sum(e1, axis=0, keepdims=True)
        n2 = lam / jnp.sum(e2, axis=0, keepdims=True)
        ot = _dot(vt_ref[p], e1.astype(BF16)) * n1 - _dot(vt_ref[p], e2.astype(BF16)) * n2
        o = ot.T
        ms = jnp.mean(o * o, axis=-1, keepdims=True)
        y = (o * lax.rsqrt(ms + EPS)) * ng_ref[:, sl] * (1.0 - lam_init)
        o_ref[:, sl] = (y * _silu(z_ref[:, sl].astype(F32))).astype(o_ref.dtype)


def _attention(proj, rope_c, rope_sa, rope_sb, lam_p, ng, layer, lam_init):
    b, s, _ = proj.shape
    bq = min(Q_BLOCK, s)
    col0 = (3 * M_WIDTH + 5 * H_WIDTH) // A_WIDTH
    seq_full = lambda c: pl.BlockSpec((None, s, A_WIDTH), lambda bi, i, c=c: (bi, 0, c))
    seq_blk = lambda c: pl.BlockSpec((None, bq, A_WIDTH), lambda bi, i, c=c: (bi, i, c))
    tab = pl.BlockSpec((None, LANES, s), lambda bi, i: (bi, 0, 0))
    return pl.pallas_call(
        functools.partial(_attn_body, lam_init=lam_init),
        grid=(b, s // bq),
        in_specs=[seq_blk(col0), seq_full(col0 + 1), seq_full(col0 + 2), seq_blk(col0 + 3),
                  tab, tab, tab,
                  pl.BlockSpec((None, 4, A_QKDIM), lambda bi, i: (layer, 0, 0)),
                  pl.BlockSpec((None, 1, A_WIDTH), lambda bi, i: (layer, 0, 0))],
        out_specs=pl.BlockSpec((None, bq, A_WIDTH), lambda bi, i: (bi, i, 0)),
        out_shape=jax.ShapeDtypeStruct((b, s, A_WIDTH), BF16),
        scratch_shapes=[pltpu.VMEM((s, A_WIDTH), BF16),
                        pltpu.VMEM((A_HEADS, A_VDIM, s), BF16),
                        pltpu.VMEM((s, LANES), F32), pltpu.VMEM((s, LANES), F32),
                        pltpu.VMEM((s, LANES), F32)],
        compiler_params=pltpu.CompilerParams(
            dimension_semantics=("parallel", "arbitrary"), vmem_limit_bytes=VMEM_LIMIT_BYTES),
        name="diff_attn",
    )(proj, proj, proj, proj, rope_c, rope_sa, rope_sb, lam_p, ng)


def _hgrn_consts():
    t = np.arange(H_CHUNK)
    tt, r = t[:, None], t[None, :]
    blocks = []
    for lvl in range(H_LEVELS):
        start = (t >> lvl) << lvl
        end = start + (1 << lvl) - 1
        is_right = ((t >> lvl) & 1) == 1
        prefix = (r >= start[:, None]) & (r <= tt)
        suffix = (r > tt) & (r <= end[:, None])
        blocks.append(np.where(is_right[:, None], prefix, suffix))
    blocks.append(r <= tt)
    blocks.append(r > tt)
    w_f = np.stack(blocks).astype(np.float32)
    w_b = w_f[:, ::-1, ::-1]
    x = tt ^ r
    top = np.where(x > 0, np.floor(np.log2(np.maximum(x, 1))), -1).astype(np.int32)
    lvl_f = np.where(tt > r, top, -1).astype(np.int32)
    nblk = H_LEVELS + 2
    return (w_f.reshape(nblk * H_CHUNK, H_CHUNK), np.ascontiguousarray(w_b).reshape(nblk * H_CHUNK, H_CHUNK),
            lvl_f, np.ascontiguousarray(lvl_f.T))


def _hgrn_chunk(q, k, g, v, vt, w_ref, lvl, st_ref, fwd):
    T = H_CHUNK
    e = _sum_halves(_dot(w_ref[...], _split_bf16(g)))
    trow = lax.broadcasted_iota(jnp.int32, (T, 1), 0)
    scores = jnp.zeros((T, T), F32)
    for lv in range(H_LEVELS):
        q_role = ((trow >> lv) & 1) == (1 if fwd else 0)
        x = (jnp.where(q_role, q, k) * jnp.exp(e[lv * T:(lv + 1) * T])).astype(BF16)
        scores = jnp.where(lvl == lv, _nt(x, x), scores)
    diag = jnp.sum(q * k, axis=-1, keepdims=True)
    o = _dot(scores.astype(BF16), v) + diag * v.astype(F32)
    e_q = e[H_LEVELS * T:(H_LEVELS + 1) * T]
    e_k = e[(H_LEVELS + 1) * T:(H_LEVELS + 2) * T]
    st = st_ref[...]
    o = o + _nt((q * jnp.exp(e_q)).astype(BF16), st.astype(BF16))
    kt = (k * jnp.exp(e_k)).astype(BF16)
    e_tot = e_q[T - 1:T] if fwd else e_q[0:1]
    st_ref[...] = st * jnp.exp(e_tot) + _dot(vt, kt)
    return o


def _hgrn_fast_blocks(chains):
    T = H_CHUNK
    assert T == 2 * H_SUB
    in_a = lax.broadcasted_iota(jnp.int32, (T, 1), 0) < H_SUB
    cums = [_sum_halves(_dot(tri, _split_bf16(g))) for (_, _, g, _, _, tri, _, _) in chains]
    q_sub, q_blk, k_inv, k_sub, k_blk, decays = [], [], [], [], [], []
    for (q, k, _, _, _, _, _, fwd), cum in zip(chains, cums):
        e_a = jnp.exp(cum[H_SUB - 1:H_SUB] if fwd else cum[0:1])
        e_b = jnp.exp(cum[T - 1:T] if fwd else cum[H_SUB:H_SUB + 1])
        in_first = in_a if fwd else jnp.logical_not(in_a)
        e_first, e_second = (e_a, e_b) if fwd else (e_b, e_a)
        qs = q * jnp.exp(cum)
        ki = k * jnp.exp(-cum)
        ks = ki * jnp.where(in_first, e_first, e_second)
        q_sub.append(qs.astype(BF16))
        q_blk.append(jnp.where(in_first, qs, qs * e_first).astype(BF16))
        k_inv.append(ki.astype(BF16))
        k_sub.append(ks.astype(BF16))
        k_blk.append(jnp.where(in_first, ks * e_second, ks).astype(BF16))
        decays.append(e_first * e_second)
    same = [_nt(qs, ki) for qs, ki in zip(q_sub, k_inv)]
    cross = [_nt(qs, ks) for qs, ks in zip(q_sub, k_sub)]
    incs = [_dot(vt, kb) for (_, _, _, _, vt, _, _, _), kb in zip(chains, k_blk)]
    outs = [_dot(jnp.where(kind == 1, sm, jnp.where(kind == 2, cr, 0.0)).astype(BF16), v)
            for (_, _, _, v, _, _, kind, _), sm, cr in zip(chains, same, cross)]
    return outs, q_blk, incs, decays


def _hgrn_body(q_ref, ff_ref, fb_ref, i_ref, z_ref, lb_ref, ng_ref, wf_ref, wb_ref, lf_ref, lbm_ref,
               tf_ref, tb_ref, mf_ref, mb_ref, o_ref, k_s, g_s, acc_s, st_s, vt_s, qt_s, inc_s, dec_s,
               *, layer):
    s = q_ref.shape[0]
    n_chunks = s // H_CHUNK
    n_sub = s // H_SUB
    lowest = None
    for d, pre_ref in enumerate((ff_ref, fb_ref)):
        z = lb_ref[d]
        p = jnp.exp(z - jnp.max(z, axis=0, keepdims=True))
        p = p / jnp.sum(p, axis=0, keepdims=True)
        lb = jnp.sum(p[1:layer + 1], axis=0, keepdims=True) if layer > 0 else jnp.zeros((1, LANES), F32)
        f = lb + (1.0 - lb) * _sigmoid(pre_ref[...].astype(F32))
        g = jnp.log(f)
        g_s[d] = g
        k_s[d] = 1.0 - f
        tot = jnp.min(jnp.sum(g.reshape(n_sub, H_SUB, LANES), axis=1))
        lowest = tot if lowest is None else jnp.minimum(lowest, tot)
    for c in range(n_chunks):
        vt_s[c] = i_ref[c * H_CHUNK:(c + 1) * H_CHUNK, :].astype(F32).T.astype(BF16)

    def block_args(c, d):
        rows = pl.ds(pl.multiple_of(c * H_CHUNK, H_CHUNK), H_CHUNK)
        return rows, (q_ref[rows, :].astype(F32), k_s[d, rows, :], g_s[d, rows, :], i_ref[rows, :], vt_s[c])

    def fast():
        tris = (tf_ref[...], tb_ref[...])
        kinds = (mf_ref[...], mb_ref[...])
        per_step = math.gcd(H_BLOCKS_PER_STEP, n_chunks)

        def intra(i, carry):
            where, chains = [], []
            for u in range(per_step):
                c = i * per_step + u
                for d in range(2):
                    rows, args = block_args(c, d)
                    where.append((c, d, rows))
                    chains.append(args + (tris[d], kinds[d], d == 0))
            outs, q_blk, incs, decays = _hgrn_fast_blocks(chains)
            for (c, d, rows), o, qb, inc, dec in zip(where, outs, q_blk, incs, decays):
                if d == 0:
                    acc_s[rows, :] = o
                else:
                    acc_s[rows, :] += o
                qt_s[d, rows, :] = qb
                inc_s[d, c] = inc
                dec_s[d, c] = dec
            return carry

        lax.fori_loop(0, n_chunks // per_step, intra, 0)

        def scan(j, states):
            new = []
            for d in range(2):
                c = j if d == 0 else n_chunks - 1 - j
                rows = pl.ds(pl.multiple_of(c * H_CHUNK, H_CHUNK), H_CHUNK)
                acc_s[rows, :] += _nt(qt_s[d, rows, :], states[d].astype(BF16))
                new.append(states[d] * dec_s[d, c] + inc_s[d, c])
            return tuple(new)

        zero = jnp.zeros((H_HEAD_DIM, H_HEAD_DIM), F32)
        lax.fori_loop(0, n_chunks, scan, (zero, zero), unroll=4)

    def safe():
        acc_s[...] = jnp.zeros_like(acc_s)
        st_s[...] = jnp.zeros_like(st_s)
        lvl_f = lf_ref[...]
        lvl_b = lbm_ref[...]

        def step(n, carry):
            for d in range(2):
                rows, args = block_args(n if d == 0 else n_chunks - 1 - n, d)
                acc_s[rows, :] += _hgrn_chunk(*args, wf_ref if d == 0 else wb_ref,
                                              lvl_f if d == 0 else lvl_b, st_s.at[d], d == 0)
            return carry

        lax.fori_loop(0, n_chunks, step, 0)

    lax.cond(lowest >= H_SAFE_LOG_DECAY, fast, safe)
    o = acc_s[...]
    ms = jnp.mean(o * o, axis=-1, keepdims=True)
    y = (o * lax.rsqrt(ms + EPS)) * ng_ref[...]
    o_ref[...] = (y * _silu(z_ref[...].astype(F32))).astype(o_ref.dtype)


def _hgrn(proj, lb_logits, ng, layer):
    b, s, _ = proj.shape
    col0 = 3 * M_WIDTH // H_HEAD_DIM
    w_f, w_b, lvl_f, lvl_b = _hgrn_consts()
    tri_f, tri_b, kind_f, kind_b = _hgrn_fast_consts()
    seq = lambda g: pl.BlockSpec((None, s, H_HEAD_DIM), lambda bi, h, g=g: (bi, 0, col0 + g * H_HEADS + h))
    const = lambda shape: pl.BlockSpec(shape, lambda bi, h: (0,) * len(shape))
    consts = (jnp.asarray(w_f, BF16), jnp.asarray(w_b, BF16), jnp.asarray(lvl_f), jnp.asarray(lvl_b),
              jnp.asarray(tri_f, BF16), jnp.asarray(tri_b, BF16), jnp.asarray(kind_f), jnp.asarray(kind_b))
    n_chunks = s // H_CHUNK
    return pl.pallas_call(
        functools.partial(_hgrn_body, layer=layer),
        grid=(b, H_HEADS),
        in_specs=[seq(0), seq(1), seq(2), seq(3), seq(4),
                  pl.BlockSpec((2, DEPTH, H_HEAD_DIM), lambda bi, h: (0, 0, h)),
                  pl.BlockSpec((None, 1, H_HEAD_DIM), lambda bi, h: (layer, 0, h))]
                 + [const(c.shape) for c in consts],
        out_specs=pl.BlockSpec((None, s, H_HEAD_DIM), lambda bi, h: (bi, 0, h)),
        out_shape=jax.ShapeDtypeStruct((b, s, H_WIDTH), BF16),
        scratch_shapes=[pltpu.VMEM((2, s, H_HEAD_DIM), F32),
                        pltpu.VMEM((2, s, H_HEAD_DIM), F32),
                        pltpu.VMEM((s, H_HEAD_DIM), F32),
                        pltpu.VMEM((2, H_HEAD_DIM, H_HEAD_DIM), F32),
                        pltpu.VMEM((s // H_CHUNK, H_HEAD_DIM, H_CHUNK), BF16),
                        pltpu.VMEM((2, s, H_HEAD_DIM), BF16),
                        pltpu.VMEM((2, n_chunks, H_HEAD_DIM, H_HEAD_DIM), F32),
                        pltpu.VMEM((2, n_chunks, 1, H_HEAD_DIM), F32)],
        compiler_params=pltpu.CompilerParams(
            dimension_semantics=("parallel", "arbitrary"), vmem_limit_bytes=VMEM_LIMIT_BYTES),
        name="hgrn2",
    )(proj, proj, proj, proj, proj, lb_logits, ng, *consts)


def _hgrn_fast_consts():
    t = np.arange(H_CHUNK)
    tt, r = t[:, None], t[None, :]
    same = (tt // H_SUB) == (r // H_SUB)
    tri_f, tri_b = same & (r <= tt), same & (r >= tt)
    kind_f = np.where(tri_f, 1, np.where((tt >= H_SUB) & (r < H_SUB), 2, 0)).astype(np.int32)
    kind_b = np.where(tri_b, 1, np.where((tt < H_SUB) & (r >= H_SUB), 2, 0)).astype(np.int32)
    return tri_f.astype(np.float32), tri_b.astype(np.float32), kind_f, kind_b


def _log_sigmoid(x):
    return jnp.minimum(x, 0.0) - jnp.log(1.0 + jnp.exp(-jnp.abs(x)))


def _mlstm_chunks(chains, gates_of):
    T = M_CHUNK
    sc = [_dot(q, kt) for (q, kt, _, _, _, _, _, _) in chains]
    qc = [_dot(q, c_ref[...].astype(BF16)) for (q, _, _, _, _, c_ref, _, _) in chains]
    gates = gates_of()
    m_news = []
    for (_, kt, vext, _, m_prev, c_ref, fwd, _), (bc, br, ir) in zip(chains, gates):
        b_last = bc[T - 1:T] if fwd else bc[0:1]
        dec = b_last - (br - ir)
        m_new = jnp.maximum(b_last + m_prev, jnp.max(dec, axis=-1, keepdims=True))
        upd = _dot((kt.astype(F32) * jnp.exp(dec - m_new)).astype(BF16), vext)
        c_ref[...] = jnp.exp(b_last + m_prev - m_new) * c_ref[...] + upd
        m_news.append(m_new)
    ws, scale = [], []
    for (_, _, _, mask, m_prev, _, _, _), (bc, br, ir), s_qk in zip(chains, gates, sc):
        a = jnp.where(mask, ir - br, -jnp.inf)
        mu = jnp.maximum(jnp.max(a, axis=-1, keepdims=True), m_prev)
        ws.append((jnp.exp(a - mu) * s_qk).astype(BF16))
        scale.append((jnp.exp(m_prev - mu), jnp.exp(-(bc + mu))))
    out = []
    for chain, w, q_c, (g_inter, floor), m_new in zip(chains, ws, qc, scale, m_news):
        vext, den_lane = chain[2], chain[7]
        nd = _dot(w, vext) + g_inter * q_c
        den = nd[:, den_lane:den_lane + 1]
        out.append((nd, jnp.maximum(jnp.abs(den), floor), m_new))
    return out


def _mlstm_body(xm_ref, om_ref, zm_ref, cw_ref, cb_ref, wq_ref, wk_ref, wkt_ref, wv_ref, gw_ref, gwt_ref,
                gb_ref, gbt_ref, skip_ref, ng_ref, tl_ref, tu_ref, o_ref,
                xc_s, q_s, kt_s, v_s, g_s, gr_s, nd_s, dd_s, c_s):
    s = xm_ref.shape[0]
    T = M_CHUNK
    n_chunks = s // T
    rb = min(M_ROW_BLOCK, s)
    halo = 16
    n_gates = 4 * M_HEADS

    lane_w = lax.broadcasted_iota(jnp.int32, (rb, M_WIN), 1)
    lane_g = lax.broadcasted_iota(jnp.int32, (rb, LANES), 1)
    gate_id = lax.broadcasted_iota(jnp.int32, (n_gates, rb), 0)
    for blk in range(s // rb):
        r0 = blk * rb
        mid = xm_ref[r0:r0 + rb, :].astype(F32)
        top = xm_ref[r0 - halo:r0, :].astype(F32) if blk > 0 else jnp.zeros((halo, M_WIDTH), F32)
        bot = (xm_ref[r0 + rb:r0 + rb + halo, :].astype(F32) if r0 + rb < s
               else jnp.zeros((halo, M_WIDTH), F32))
        xp = jnp.concatenate([top, mid, bot], axis=0)
        n_rows = rb + 2 * halo
        conv = jnp.zeros((rb, M_WIDTH), F32) + cb_ref[...]
        for j in range(M_CONV):
            shift = (M_CONV // 2 - j) % n_rows
            sh = xp if shift == 0 else pltpu.roll(xp, shift, 0)
            conv = conv + sh[halo:halo + rb] * cw_ref[j:j + 1, :]
        xc = _silu(conv)
        xc_s[r0:r0 + rb, :] = xc
        xcb = xc.astype(BF16)
        gates = jnp.zeros((rb, LANES), F32) + gb_ref[...]
        gates_t = jnp.zeros((n_gates, rb), F32) + gbt_ref[...]
        for h in range(M_HEADS):
            a = M_WIN_START[h]
            xw = xcb[:, a:a + M_WIN]
            qh = _dot(xw, wq_ref[h])
            kh = _dot(xw, wk_ref[h])
            kht = _nt(wkt_ref[h], xw)
            vh = _dot(xm_ref[r0:r0 + rb, a:a + M_WIN], wv_ref[h])
            qb, kb, vb = qh.astype(BF16), kh.astype(BF16), vh.astype(BF16)
            gates = gates + _dot(qb, gw_ref[0, h]) + _dot(kb, gw_ref[1, h]) + _dot(vb, gw_ref[2, h])
            gates_t = gates_t + _nt(gwt_ref[0, h], qb) + _nt(gwt_ref[1, h], kb) + _nt(gwt_ref[2, h], vb)
            q_s[h, r0:r0 + rb, :] = (qh * (M_HEAD_DIM ** -0.5)).astype(BF16)
            for a_c in range(rb // T):
                kt_s[h, r0 // T + a_c] = kht[:, a_c * T:(a_c + 1) * T].astype(BF16)
            v_s[h, r0:r0 + rb, :] = jnp.where(lane_w == M_DEN_LANE[h], 1.0, vh).astype(BF16)
        sp = _split_bf16(_log_sigmoid(gates))
        cum_f = _sum_halves(_dot(tl_ref[...], sp))
        cum_b = _sum_halves(_dot(tu_ref[...], sp))
        g_s[r0:r0 + rb, :] = jnp.where(lane_g >= 3 * M_HEADS, cum_b, cum_f)
        lf = _log_sigmoid(gates_t)
        hi = lf.astype(BF16)
        lo = (lf - hi.astype(F32)).astype(BF16)
        both = jnp.concatenate([hi, lo], axis=0)
        row_f = _dot(both, tu_ref[...])
        row_b = _dot(both, tl_ref[...])
        rows_t = jnp.where((gate_id >= M_HEADS) & (gate_id < 2 * M_HEADS), row_f[:n_gates] + row_f[n_gates:],
                           jnp.where(gate_id >= 3 * M_HEADS, row_b[:n_gates] + row_b[n_gates:], gates_t))
        for a_c in range(rb // T):
            gr_s[r0 // T + a_c] = rows_t[:, a_c * T:(a_c + 1) * T]

    c_s[...] = jnp.zeros_like(c_s)
    tt = lax.broadcasted_iota(jnp.int32, (T, T), 0)
    ss = lax.broadcasted_iota(jnp.int32, (T, T), 1)

    def step(n, ms):
        chains, where = [], []
        for d in range(2):
            fwd = d == 0
            c = n if fwd else n_chunks - 1 - n
            rows = pl.ds(pl.multiple_of(c * T, T), T)
            mask = (ss <= tt) if fwd else (ss >= tt)
            for h in range(M_HEADS):
                idx = d * M_HEADS + h
                chains.append((q_s[h, rows, :], kt_s[h, c], v_s[h, rows, :], mask, ms[idx], c_s.at[idx],
                               fwd, M_DEN_LANE[h]))
                where.append((h, rows, c))

        def gates_of():
            out = []
            for d in range(2):
                fwd = d == 0
                _, rows, c = where[d * M_HEADS]
                col = g_s[rows, :]
                row = gr_s[c]
                f0 = M_HEADS if fwd else 3 * M_HEADS
                i0 = 0 if fwd else 2 * M_HEADS
                for h in range(M_HEADS):
                    out.append((col[:, f0 + h:f0 + h + 1], row[f0 + h:f0 + h + 1, :],
                                row[i0 + h:i0 + h + 1, :]))
            return out

        new_ms = []
        for idx, ((h, rows, _), (nd, dd, m_new)) in enumerate(zip(where, _mlstm_chunks(chains, gates_of))):
            nd_s[idx // M_HEADS, h, rows, :] = nd.astype(BF16)
            dd_s[rows, idx:idx + 1] = dd
            new_ms.append(m_new)
        return tuple(new_ms)

    lax.fori_loop(0, n_chunks, step, tuple(jnp.zeros((1, 1), F32) for _ in range(2 * M_HEADS)))

    for blk in range(s // rb):
        r0 = blk * rb
        inv_dd = 1.0 / dd_s[r0:r0 + rb, 0:2 * M_HEADS]
        for h in range(M_HEADS):
            a = M_WIN_START[h]
            valid = (lane_w >= M_WIN_OFF[h]) & (lane_w < M_WIN_OFF[h] + M_HEAD_DIM)
            hsum = (nd_s[0, h, r0:r0 + rb, :].astype(F32) * inv_dd[:, h:h + 1]
                    + nd_s[1, h, r0:r0 + rb, :].astype(F32) * inv_dd[:, M_HEADS + h:M_HEADS + h + 1])
            hm = _gate_sigmoid(om_ref[r0:r0 + rb, a:a + M_WIN].astype(F32)) * jnp.where(valid, hsum, 0.0)
            ms = jnp.sum(hm * hm, axis=-1, keepdims=True) * (1.0 / M_HEAD_DIM)
            y = (hm * lax.rsqrt(ms + EPS)) * ng_ref[h] + skip_ref[h] * xc_s[r0:r0 + rb, a:a + M_WIN]
            y = y * _silu(zm_ref[r0:r0 + rb, a:a + M_WIN].astype(F32))
            o_ref[r0:r0 + rb, h * M_WIN:(h + 1) * M_WIN] = y.astype(o_ref.dtype)


def _mlstm(proj, params, layer):
    b, s, _ = proj.shape
    rb = min(M_ROW_BLOCK, s)
    t = np.arange(rb)
    tri = ((t[:, None] // M_CHUNK == t[None, :] // M_CHUNK) & (t[None, :] <= t[:, None])).astype(np.float32)
    seq = lambda c: pl.BlockSpec((None, s, M_WIDTH), lambda bi, c=c: (bi, 0, c))
    const = lambda shape: pl.BlockSpec(shape, lambda bi: (0,) * len(shape), pipeline_mode=pl.Buffered(1))
    of_layer = lambda a: pl.BlockSpec((None,) + a.shape[1:], lambda bi: (layer,) + (0,) * (a.ndim - 1),
                                      pipeline_mode=pl.Buffered(1))
    tris = (jnp.asarray(tri, BF16), jnp.asarray(tri.T, BF16))
    return pl.pallas_call(
        _mlstm_body,
        grid=(b,),
        in_specs=[seq(0), seq(1), seq(2)] + [of_layer(a) for a in params] + [const(t.shape) for t in tris],
        out_specs=pl.BlockSpec((None, s, M_HEADS * M_WIN), lambda bi: (bi, 0, 0), pipeline_mode=pl.Buffered(1)),
        out_shape=jax.ShapeDtypeStruct((b, s, M_HEADS * M_WIN), BF16),
        scratch_shapes=[
            pltpu.VMEM((s, M_WIDTH), F32),
            pltpu.VMEM((M_HEADS, s, M_WIN), BF16),
            pltpu.VMEM((M_HEADS, s // M_CHUNK, M_WIN, M_CHUNK), BF16),
            pltpu.VMEM((M_HEADS, s, M_WIN), BF16),
            pltpu.VMEM((s, LANES), F32),
            pltpu.VMEM((s // M_CHUNK, 4 * M_HEADS, M_CHUNK), F32),
            pltpu.VMEM((2, M_HEADS, s, M_WIN), BF16),
            pltpu.VMEM((s, LANES), F32),
            pltpu.VMEM((2 * M_HEADS, M_WIN, M_WIN), F32),
        ],
        compiler_params=pltpu.CompilerParams(
            dimension_semantics=("parallel",), vmem_limit_bytes=M_VMEM_LIMIT_BYTES),
        name="mlstm",
    )(proj, proj, proj, *params, *tris)


def _mlstm_params(m_conv_w, m_conv_b, m_wq, m_wk, m_wv, m_w_gates, m_b_gates, m_skip, m_norm_g, w_out_m):
    def dense(w):
        eye = jnp.eye(M_WIDTH // M_QKV_BLOCK, dtype=F32)
        return jnp.einsum("gh,gio->giho", eye, w).reshape(M_WIDTH, M_WIDTH)

    def head_pad(h, at_offset):
        lo = M_WIN_OFF[h] if at_offset else 0
        return (lo, M_WIN - M_HEAD_DIM - lo)

    def head_weights(w, at_offset):
        d = dense(w)
        out = []
        for h in range(M_HEADS):
            blk = d[M_WIN_START[h]:M_WIN_START[h] + M_WIN, h * M_HEAD_DIM:(h + 1) * M_HEAD_DIM]
            out.append(jnp.pad(blk, ((0, 0), head_pad(h, at_offset))))
        return jnp.stack(out).astype(BF16)

    def window(vec):
        out = []
        for h in range(M_HEADS):
            head = vec[h * M_HEAD_DIM:(h + 1) * M_HEAD_DIM]
            out.append(jnp.pad(head, (M_WIN_OFF[h], M_WIN - M_HEAD_DIM - M_WIN_OFF[h])))
        return jnp.stack(out)[:, None, :]

    n_gates = 4 * M_HEADS
    gw4 = m_w_gates.reshape(3, M_HEADS, M_HEAD_DIM, n_gates)
    gw = jnp.stack([
        jnp.stack([jnp.pad(gw4[x, h], (head_pad(h, x == 2), (0, 0))) for h in range(M_HEADS)])
        for x in range(3)])
    gwt = jnp.swapaxes(gw, -1, -2).astype(BF16)
    gw = jnp.pad(gw, ((0, 0), (0, 0), (0, 0), (0, LANES - n_gates))).astype(BF16)
    gb = jnp.pad(m_b_gates, (0, LANES - n_gates))[None, :]
    wo = []
    for h in range(M_HEADS):
        rows = w_out_m[h * M_HEAD_DIM:(h + 1) * M_HEAD_DIM]
        wo.append(jnp.pad(rows, (head_pad(h, True), (0, 0))))
    wk = head_weights(m_wk, False)
    return (m_conv_w, m_conv_b[None, :], head_weights(m_wq, False), wk, jnp.swapaxes(wk, 1, 2),
            head_weights(m_wv, True), gw, gwt, gb, m_b_gates[:, None], window(m_skip),
            window(m_norm_g)), jnp.concatenate(wo, axis=0).astype(BF16)


def _rope_tables(positions):
    half = ROPE_DIM // 2
    d = (jnp.arange(LANES) % A_QKDIM)[:, None]
    inv = jnp.where(d < ROPE_DIM, ROPE_THETA ** (-(d % half).astype(F32) / half), 0.0)
    ang = positions.astype(F32)[:, None, :] * inv
    cos, sin = jnp.cos(ang), jnp.sin(ang)
    return (jnp.where(d < ROPE_DIM, cos, 1.0), jnp.where(d < half, -sin, 0.0),
            jnp.where((d >= half) & (d < ROPE_DIM), sin, 0.0))


def kernel(x, positions, norm_g, w_in, m_conv_w, m_conv_b, m_wq, m_wk, m_wv, m_w_gates, m_b_gates,
           m_skip, m_norm_g, h_lb_logits, h_norm_g, a_lambda, a_norm_g, w_out, final_g):
    b, s, _ = x.shape
    rope_c, rope_sa, rope_sb = _rope_tables(positions)
    w_in_b = w_in.astype(BF16)
    w_out_b = w_out.astype(BF16)
    m_params, w_out_m = jax.vmap(_mlstm_params)(m_conv_w, m_conv_b, m_wq, m_wk, m_wv, m_w_gates, m_b_gates,
                                                m_skip, m_norm_g, w_out[:, :M_WIDTH])
    lb_logits = jnp.swapaxes(h_lb_logits, 0, 1)
    x2d = x.reshape(b * s, D_MODEL)
    for l in range(DEPTH):
        proj = _inproj(x2d, norm_g[:, None, :], w_in_b, l).reshape(b, s, IN_COLS)
        y_m = _mlstm(proj, m_params, l)
        y_h = _hgrn(proj, lb_logits, h_norm_g[:, None, :], l)
        lam_init = 0.8 - 0.6 * math.exp(-0.3 * l)
        y_a = _attention(proj, rope_c, rope_sa, rope_sb, a_lambda, a_norm_g[:, None, :], l, lam_init)
        x2d = _outproj(x2d, y_m.reshape(b * s, -1), y_h.reshape(b * s, -1), y_a.reshape(b * s, -1),
                       w_out_m, w_out_b, final_g[None, :], l, l == DEPTH - 1)
    return x2d.reshape(b, s, D_MODEL)
```

```python
import functools
import math

import numpy as np
import jax
import jax.numpy as jnp
from jax import lax
from jax.experimental import pallas as pl
from jax.experimental.pallas import tpu as pltpu

F32 = jnp.float32
BF16 = jnp.bfloat16

D_MODEL = 1024
DEPTH = 4
M_WIDTH = 768
M_HEADS = 4
M_HEAD_DIM = 192
M_QKV_BLOCK = 4
M_CONV = 5
H_WIDTH = 768
H_HEAD_DIM = 128
H_HEADS = 6
A_WIDTH = 512
A_HEADS = 4
A_VDIM = 128
A_QKDIM = 64
ROPE_DIM = 16
ROPE_THETA = 500000.0
IN_COLS = 3 * M_WIDTH + 5 * H_WIDTH + 4 * A_WIDTH
EPS = 1e-6

LANES = 128
MXU_DIM = 256
VMEM_LIMIT_BYTES = 56 * 1024 * 1024
M_VMEM_LIMIT_BYTES = 60 * 1024 * 1024

M_WIN = MXU_DIM
M_WIN_START = (0, 128, 384, 512)
M_WIN_OFF = (0, 64, 0, 64)
M_CHUNK = 128
M_ROW_BLOCK = 256
H_CHUNK = 128
H_LEVELS = 7
H_SUB = 64
H_SAFE_LOG_DECAY = -80.0
H_BLOCKS_PER_STEP = 4
M_DEN_LANE = (192, 0, 192, 0)
Q_BLOCK = 256


def _nt(a, b):
    return lax.dot_general(a, b, (((1,), (1,)), ((), ())), preferred_element_type=F32)


def _tn(a, b):
    return lax.dot_general(a, b, (((0,), (0,)), ((), ())), preferred_element_type=F32)


def _dot(a, b):
    return jnp.dot(a, b, preferred_element_type=F32)


def _sigmoid(x):
    return 1.0 / (1.0 + jnp.exp(-x))


def _gate_sigmoid(x):
    return 0.5 * jnp.tanh(0.5 * x) + 0.5


def _silu(x):
    return x * _gate_sigmoid(x)


def _split_bf16(x):
    hi = x.astype(BF16)
    lo = (x - hi.astype(F32)).astype(BF16)
    return jnp.concatenate([hi, lo], axis=1)


def _sum_halves(r):
    return r[:, :LANES] + r[:, LANES:]


def _inproj_body(x_ref, g_ref, w_ref, o_ref, h_ref):
    @pl.when(pl.program_id(1) == 0)
    def _():
        xf = x_ref[...]
        ms = jnp.mean(xf * xf, axis=-1, keepdims=True)
        h_ref[...] = ((xf * lax.rsqrt(ms + EPS)) * g_ref[...]).astype(BF16)

    o_ref[...] = _dot(h_ref[...], w_ref[...]).astype(o_ref.dtype)


def _inproj(x2d, g, w, layer):
    m = x2d.shape[0]
    bm = min(1024, m)
    bn = 4096
    return pl.pallas_call(
        _inproj_body,
        grid=(m // bm, IN_COLS // bn),
        in_specs=[
            pl.BlockSpec((bm, D_MODEL), lambda i, j: (i, 0)),
            pl.BlockSpec((None, 1, D_MODEL), lambda i, j: (layer, 0, 0)),
            pl.BlockSpec((None, D_MODEL, bn), lambda i, j: (layer, 0, j)),
        ],
        out_specs=pl.BlockSpec((bm, bn), lambda i, j: (i, j)),
        out_shape=jax.ShapeDtypeStruct((m, IN_COLS), BF16),
        scratch_shapes=[pltpu.VMEM((bm, D_MODEL), BF16)],
        compiler_params=pltpu.CompilerParams(
            dimension_semantics=("parallel", "arbitrary"), vmem_limit_bytes=VMEM_LIMIT_BYTES),
        name="inproj",
    )(x2d, g, w)


def _outproj_body(x_ref, ym_ref, yh_ref, ya_ref, wm_ref, wh_ref, wa_ref, fg_ref, o_ref, *, final):
    acc = x_ref[...] + _dot(ym_ref[...], wm_ref[...])
    acc = acc + _dot(yh_ref[...], wh_ref[...])
    acc = acc + _dot(ya_ref[...], wa_ref[...])
    if final:
        ms = jnp.mean(acc * acc, axis=-1, keepdims=True)
        acc = (acc * lax.rsqrt(ms + EPS)) * fg_ref[...]
    o_ref[...] = acc


def _outproj(x2d, ym, yh, ya, wm, w_out, fg, layer, final):
    m = x2d.shape[0]
    bm = min(1024, m)
    km, kh, ka = ym.shape[1], yh.shape[1], ya.shape[1]
    return pl.pallas_call(
        functools.partial(_outproj_body, final=final),
        grid=(m // bm,),
        in_specs=[
            pl.BlockSpec((bm, D_MODEL), lambda i: (i, 0)),
            pl.BlockSpec((bm, km), lambda i: (i, 0)),
            pl.BlockSpec((bm, kh), lambda i: (i, 0)),
            pl.BlockSpec((bm, ka), lambda i: (i, 0)),
            pl.BlockSpec((None, km, D_MODEL), lambda i: (layer, 0, 0)),
            pl.BlockSpec((None, kh, D_MODEL), lambda i: (layer, M_WIDTH // kh, 0)),
            pl.BlockSpec((None, ka, D_MODEL), lambda i: (layer, (M_WIDTH + H_WIDTH) // ka, 0)),
            pl.BlockSpec((1, D_MODEL), lambda i: (0, 0)),
        ],
        out_specs=pl.BlockSpec((bm, D_MODEL), lambda i: (i, 0)),
        out_shape=jax.ShapeDtypeStruct((m, D_MODEL), F32),
        input_output_aliases={0: 0} if layer > 0 else {},
        compiler_params=pltpu.CompilerParams(
            dimension_semantics=("parallel",), vmem_limit_bytes=VMEM_LIMIT_BYTES),
        name="outproj",
    )(x2d, ym, yh, ya, wm, w_out, w_out, fg)


def _rope(x, c, sa, sb):
    return x * c + pltpu.roll(x, LANES - ROPE_DIM // 2, 1) * sa + pltpu.roll(x, ROPE_DIM // 2, 1) * sb


def _attn_body(q_ref, k_ref, v_ref, z_ref, pos_ref, inv_ref, lam_ref, ng_ref, o_ref, kr_ref, vt_ref,
               c_ref, sa_ref, sb_ref, *, lam_init):
    i = pl.program_id(1)
    bq = q_ref.shape[0]
    s = k_ref.shape[0]

    @pl.when(i == 0)
    def _():
        ang = inv_ref[...] * pos_ref[...]
        cos, sin = jnp.cos(ang), jnp.sin(ang)
        half = ROPE_DIM // 2
        ones = jnp.ones((A_QKDIM - ROPE_DIM, s), F32)
        zeros = lambda n: jnp.zeros((n, s), F32)
        two = lambda parts: jnp.concatenate(parts + parts, axis=0)
        tables = (two([cos, cos, ones]), two([-sin, zeros(A_QKDIM - half)]),
                  two([zeros(half), sin, zeros(A_QKDIM - ROPE_DIM)]))
        for src, dst in zip(tables, (c_ref, sa_ref, sb_ref)):
            for blk in range(s // LANES):
                rows = slice(blk * LANES, (blk + 1) * LANES)
                dst[rows, :] = src[:, rows].T
        c, sa, sb = c_ref[...], sa_ref[...], sb_ref[...]
        for p in range(A_HEADS):
            sl = slice(p * LANES, (p + 1) * LANES)
            kr_ref[:, sl] = _rope(k_ref[:, sl].astype(F32), c, sa, sb).astype(BF16)
            for blk in range(s // bq):
                rows = slice(blk * bq, (blk + 1) * bq)
                vt_ref[p, :, rows] = v_ref[rows, sl].astype(F32).T.astype(BF16)

    lp = lam_ref[...]
    lam = (jnp.exp(jnp.sum(lp[0:1] * lp[1:2], axis=-1, keepdims=True))
           - jnp.exp(jnp.sum(lp[2:3] * lp[3:4], axis=-1, keepdims=True)) + lam_init)

    r0 = pl.multiple_of(i * bq, bq)
    c = c_ref[pl.ds(r0, bq), :]
    sa = sa_ref[pl.ds(r0, bq), :]
    sb = sb_ref[pl.ds(r0, bq), :]
    first = lax.broadcasted_iota(jnp.int32, (LANES, bq), 0) < A_QKDIM

    def scores(p):
        sl = slice(p * LANES, (p + 1) * LANES)
        qt = (_rope(q_ref[:, sl].astype(F32), c, sa, sb) * (A_QKDIM ** -0.5 * math.log2(math.e))).T
        ks = kr_ref[:, sl]
        return (_dot(ks, jnp.where(first, qt, 0.0).astype(BF16)),
                _dot(ks, jnp.where(first, 0.0, qt).astype(BF16)))

    nxt = scores(0)
    for p in range(A_HEADS):
        sl = slice(p * LANES, (p + 1) * LANES)
        s1, s2 = nxt
        if p + 1 < A_HEADS:
            nxt = scores(p + 1)
        e1 = jnp.exp2(s1 - jnp.max(s1, axis=0, keepdims=True))
        e2 = jnp.exp2(s2 - jnp.max(s2, axis=0, keepdims=True))
        n1 = 1.0 / jnp.sum(e1, axis=0, keepdims=True)
        n2 = lam / jnp.sum(e2, axis=0, keepdims=True)
        ot = _dot(vt_ref[p], e1.astype(BF16)) * n1 - _dot(vt_ref[p], e2.astype(BF16)) * n2
        o = ot.T
        ms = jnp.mean(o * o, axis=-1, keepdims=True)
        y = (o * lax.rsqrt(ms + EPS)) * ng_ref[:, sl] * (1.0 - lam_init)
        o_ref[:, sl] = (y * _silu(z_ref[:, sl].astype(F32))).astype(o_ref.dtype)


def _attention(proj, pos, inv, lam_p, ng, layer, lam_init):
    b, s, _ = proj.shape
    bq = min(Q_BLOCK, s)
    col0 = (3 * M_WIDTH + 5 * H_WIDTH) // A_WIDTH
    seq_full = lambda c: pl.BlockSpec((None, s, A_WIDTH), lambda bi, i, c=c: (bi, 0, c))
    seq_blk = lambda c: pl.BlockSpec((None, bq, A_WIDTH), lambda bi, i, c=c: (bi, i, c))
    return pl.pallas_call(
        functools.partial(_attn_body, lam_init=lam_init),
        grid=(b, s // bq),
        in_specs=[seq_blk(col0), seq_full(col0 + 1), seq_full(col0 + 2), seq_blk(col0 + 3),
                  pl.BlockSpec((None, 1, s), lambda bi, i: (bi, 0, 0)),
                  pl.BlockSpec((ROPE_DIM // 2, 1), lambda bi, i: (0, 0)),
                  pl.BlockSpec((None, 4, A_QKDIM), lambda bi, i: (layer, 0, 0)),
                  pl.BlockSpec((None, 1, A_WIDTH), lambda bi, i: (layer, 0, 0))],
        out_specs=pl.BlockSpec((None, bq, A_WIDTH), lambda bi, i: (bi, i, 0)),
        out_shape=jax.ShapeDtypeStruct((b, s, A_WIDTH), BF16),
        scratch_shapes=[pltpu.VMEM((s, A_WIDTH), BF16),
                        pltpu.VMEM((A_HEADS, A_VDIM, s), BF16),
                        pltpu.VMEM((s, LANES), F32), pltpu.VMEM((s, LANES), F32),
                        pltpu.VMEM((s, LANES), F32)],
        compiler_params=pltpu.CompilerParams(
            dimension_semantics=("parallel", "arbitrary"), vmem_limit_bytes=VMEM_LIMIT_BYTES),
        name="diff_attn",
    )(proj, proj, proj, proj, pos, inv, lam_p, ng)


def _hgrn_consts():
    t = np.arange(H_CHUNK)
    tt, r = t[:, None], t[None, :]
    blocks = []
    for lvl in range(H_LEVELS):
        start = (t >> lvl) << lvl
        end = start + (1 << lvl) - 1
        is_right = ((t >> lvl) & 1) == 1
        prefix = (r >= start[:, None]) & (r <= tt)
        suffix = (r > tt) & (r <= end[:, None])
        blocks.append(np.where(is_right[:, None], prefix, suffix))
    blocks.append(r <= tt)
    blocks.append(r > tt)
    w_f = np.stack(blocks).astype(np.float32)
    w_b = w_f[:, ::-1, ::-1]
    x = tt ^ r
    top = np.where(x > 0, np.floor(np.log2(np.maximum(x, 1))), -1).astype(np.int32)
    lvl_f = np.where(tt > r, top, -1).astype(np.int32)
    nblk = H_LEVELS + 2
    return (w_f.reshape(nblk * H_CHUNK, H_CHUNK), np.ascontiguousarray(w_b).reshape(nblk * H_CHUNK, H_CHUNK),
            lvl_f, np.ascontiguousarray(lvl_f.T))


def _hgrn_chunk(q, k, g, v, vt, w_ref, lvl, st_ref, fwd):
    T = H_CHUNK
    e = _sum_halves(_dot(w_ref[...], _split_bf16(g)))
    trow = lax.broadcasted_iota(jnp.int32, (T, 1), 0)
    scores = jnp.zeros((T, T), F32)
    for lv in range(H_LEVELS):
        q_role = ((trow >> lv) & 1) == (1 if fwd else 0)
        x = (jnp.where(q_role, q, k) * jnp.exp(e[lv * T:(lv + 1) * T])).astype(BF16)
        scores = jnp.where(lvl == lv, _nt(x, x), scores)
    diag = jnp.sum(q * k, axis=-1, keepdims=True)
    o = _dot(scores.astype(BF16), v) + diag * v.astype(F32)
    e_q = e[H_LEVELS * T:(H_LEVELS + 1) * T]
    e_k = e[(H_LEVELS + 1) * T:(H_LEVELS + 2) * T]
    st = st_ref[...]
    o = o + _nt((q * jnp.exp(e_q)).astype(BF16), st.astype(BF16))
    kt = (k * jnp.exp(e_k)).astype(BF16)
    e_tot = e_q[T - 1:T] if fwd else e_q[0:1]
    st_ref[...] = st * jnp.exp(e_tot) + _dot(vt, kt)
    return o


def _hgrn_fast_blocks(chains):
    T = H_CHUNK
    assert T == 2 * H_SUB
    in_a = lax.broadcasted_iota(jnp.int32, (T, 1), 0) < H_SUB
    cums = [_sum_halves(_dot(tri, _split_bf16(g))) for (_, _, g, _, _, tri, _, _) in chains]
    q_sub, q_blk, k_inv, k_sub, k_blk, decays = [], [], [], [], [], []
    for (q, k, _, _, _, _, _, fwd), cum in zip(chains, cums):
        e_a = jnp.exp(cum[H_SUB - 1:H_SUB] if fwd else cum[0:1])
        e_b = jnp.exp(cum[T - 1:T] if fwd else cum[H_SUB:H_SUB + 1])
        in_first = in_a if fwd else jnp.logical_not(in_a)
        e_first, e_second = (e_a, e_b) if fwd else (e_b, e_a)
        qs = q * jnp.exp(cum)
        ki = k * jnp.exp(-cum)
        ks = ki * jnp.where(in_first, e_first, e_second)
        q_sub.append(qs.astype(BF16))
        q_blk.append(jnp.where(in_first, qs, qs * e_first).astype(BF16))
        k_inv.append(ki.astype(BF16))
        k_sub.append(ks.astype(BF16))
        k_blk.append(jnp.where(in_first, ks * e_second, ks).astype(BF16))
        decays.append(e_first * e_second)
    same = [_nt(qs, ki) for qs, ki in zip(q_sub, k_inv)]
    cross = [_nt(qs, ks) for qs, ks in zip(q_sub, k_sub)]
    incs = [_dot(vt, kb) for (_, _, _, _, vt, _, _, _), kb in zip(chains, k_blk)]
    outs = [_dot(jnp.where(kind == 1, sm, jnp.where(kind == 2, cr, 0.0)).astype(BF16), v)
            for (_, _, _, v, _, _, kind, _), sm, cr in zip(chains, same, cross)]
    return outs, q_blk, incs, decays


def _hgrn_body(q_ref, ff_ref, fb_ref, i_ref, z_ref, lb_ref, ng_ref, wf_ref, wb_ref, lf_ref, lbm_ref,
               tf_ref, tb_ref, mf_ref, mb_ref, o_ref, k_s, g_s, acc_s, st_s, vt_s, qt_s, inc_s, dec_s,
               *, layer):
    s = q_ref.shape[0]
    n_chunks = s // H_CHUNK
    n_sub = s // H_SUB
    lowest = None
    for d, pre_ref in enumerate((ff_ref, fb_ref)):
        z = lb_ref[d]
        p = jnp.exp(z - jnp.max(z, axis=0, keepdims=True))
        p = p / jnp.sum(p, axis=0, keepdims=True)
        lb = jnp.sum(p[1:layer + 1], axis=0, keepdims=True) if layer > 0 else jnp.zeros((1, LANES), F32)
        f = lb + (1.0 - lb) * _sigmoid(pre_ref[...].astype(F32))
        g = jnp.log(f)
        g_s[d] = g
        k_s[d] = 1.0 - f
        tot = jnp.min(jnp.sum(g.reshape(n_sub, H_SUB, LANES), axis=1))
        lowest = tot if lowest is None else jnp.minimum(lowest, tot)
    for c in range(n_chunks):
        vt_s[c] = i_ref[c * H_CHUNK:(c + 1) * H_CHUNK, :].astype(F32).T.astype(BF16)

    def block_args(c, d):
        rows = pl.ds(pl.multiple_of(c * H_CHUNK, H_CHUNK), H_CHUNK)
        return rows, (q_ref[rows, :].astype(F32), k_s[d, rows, :], g_s[d, rows, :], i_ref[rows, :], vt_s[c])

    def fast():
        tris = (tf_ref[...], tb_ref[...])
        kinds = (mf_ref[...], mb_ref[...])
        per_step = math.gcd(H_BLOCKS_PER_STEP, n_chunks)

        def intra(i, carry):
            where, chains = [], []
            for u in range(per_step):
                c = i * per_step + u
                for d in range(2):
                    rows, args = block_args(c, d)
                    where.append((c, d, rows))
                    chains.append(args + (tris[d], kinds[d], d == 0))
            outs, q_blk, incs, decays = _hgrn_fast_blocks(chains)
            for (c, d, rows), o, qb, inc, dec in zip(where, outs, q_blk, incs, decays):
                if d == 0:
                    acc_s[rows, :] = o
                else:
                    acc_s[rows, :] += o
                qt_s[d, rows, :] = qb
                inc_s[d, c] = inc
                dec_s[d, c] = dec
            return carry

        lax.fori_loop(0, n_chunks // per_step, intra, 0)

        def scan(j, states):
            new = []
            for d in range(2):
                c = j if d == 0 else n_chunks - 1 - j
                rows = pl.ds(pl.multiple_of(c * H_CHUNK, H_CHUNK), H_CHUNK)
                acc_s[rows, :] += _nt(qt_s[d, rows, :], states[d].astype(BF16))
                new.append(states[d] * dec_s[d, c] + inc_s[d, c])
            return tuple(new)

        zero = jnp.zeros((H_HEAD_DIM, H_HEAD_DIM), F32)
        lax.fori_loop(0, n_chunks, scan, (zero, zero), unroll=4)

    def safe():
        acc_s[...] = jnp.zeros_like(acc_s)
        st_s[...] = jnp.zeros_like(st_s)
        lvl_f = lf_ref[...]
        lvl_b = lbm_ref[...]

        def step(n, carry):
            for d in range(2):
                rows, args = block_args(n if d == 0 else n_chunks - 1 - n, d)
                acc_s[rows, :] += _hgrn_chunk(*args, wf_ref if d == 0 else wb_ref,
                                              lvl_f if d == 0 else lvl_b, st_s.at[d], d == 0)
            return carry

        lax.fori_loop(0, n_chunks, step, 0)

    lax.cond(lowest >= H_SAFE_LOG_DECAY, fast, safe)
    o = acc_s[...]
    ms = jnp.mean(o * o, axis=-1, keepdims=True)
    y = (o * lax.rsqrt(ms + EPS)) * ng_ref[...]
    o_ref[...] = (y * _silu(z_ref[...].astype(F32))).astype(o_ref.dtype)


def _hgrn(proj, lb_logits, ng, layer):
    b, s, _ = proj.shape
    col0 = 3 * M_WIDTH // H_HEAD_DIM
    w_f, w_b, lvl_f, lvl_b = _hgrn_consts()
    tri_f, tri_b, kind_f, kind_b = _hgrn_fast_consts()
    seq = lambda g: pl.BlockSpec((None, s, H_HEAD_DIM), lambda bi, h, g=g: (bi, 0, col0 + g * H_HEADS + h))
    const = lambda shape: pl.BlockSpec(shape, lambda bi, h: (0,) * len(shape))
    consts = (jnp.asarray(w_f, BF16), jnp.asarray(w_b, BF16), jnp.asarray(lvl_f), jnp.asarray(lvl_b),
              jnp.asarray(tri_f, BF16), jnp.asarray(tri_b, BF16), jnp.asarray(kind_f), jnp.asarray(kind_b))
    n_chunks = s // H_CHUNK
    return pl.pallas_call(
        functools.partial(_hgrn_body, layer=layer),
        grid=(b, H_HEADS),
        in_specs=[seq(0), seq(1), seq(2), seq(3), seq(4),
                  pl.BlockSpec((2, DEPTH, H_HEAD_DIM), lambda bi, h: (0, 0, h)),
                  pl.BlockSpec((None, 1, H_HEAD_DIM), lambda bi, h: (layer, 0, h))]
                 + [const(c.shape) for c in consts],
        out_specs=pl.BlockSpec((None, s, H_HEAD_DIM), lambda bi, h: (bi, 0, h)),
        out_shape=jax.ShapeDtypeStruct((b, s, H_WIDTH), BF16),
        scratch_shapes=[pltpu.VMEM((2, s, H_HEAD_DIM), F32),
                        pltpu.VMEM((2, s, H_HEAD_DIM), F32),
                        pltpu.VMEM((s, H_HEAD_DIM), F32),
                        pltpu.VMEM((2, H_HEAD_DIM, H_HEAD_DIM), F32),
                        pltpu.VMEM((s // H_CHUNK, H_HEAD_DIM, H_CHUNK), BF16),
                        pltpu.VMEM((2, s, H_HEAD_DIM), BF16),
                        pltpu.VMEM((2, n_chunks, H_HEAD_DIM, H_HEAD_DIM), F32),
                        pltpu.VMEM((2, n_chunks, 1, H_HEAD_DIM), F32)],
        compiler_params=pltpu.CompilerParams(
            dimension_semantics=("parallel", "arbitrary"), vmem_limit_bytes=VMEM_LIMIT_BYTES),
        name="hgrn2",
    )(proj, proj, proj, proj, proj, lb_logits, ng, *consts)


def _hgrn_fast_consts():
    t = np.arange(H_CHUNK)
    tt, r = t[:, None], t[None, :]
    same = (tt // H_SUB) == (r // H_SUB)
    tri_f, tri_b = same & (r <= tt), same & (r >= tt)
    kind_f = np.where(tri_f, 1, np.where((tt >= H_SUB) & (r < H_SUB), 2, 0)).astype(np.int32)
    kind_b = np.where(tri_b, 1, np.where((tt < H_SUB) & (r >= H_SUB), 2, 0)).astype(np.int32)
    return tri_f.astype(np.float32), tri_b.astype(np.float32), kind_f, kind_b


def _log_sigmoid(x):
    return jnp.minimum(x, 0.0) - jnp.log(1.0 + jnp.exp(-jnp.abs(x)))


def _mlstm_chunks(chains, gates_of):
    T = M_CHUNK
    sc = [_dot(q, kt) for (q, kt, _, _, _, _, _, _) in chains]
    qc = [_dot(q, c_ref[...].astype(BF16)) for (q, _, _, _, _, c_ref, _, _) in chains]
    gates = gates_of()
    m_news = []
    for (_, kt, vext, _, m_prev, c_ref, fwd, _), (bc, br, ir) in zip(chains, gates):
        b_last = bc[T - 1:T] if fwd else bc[0:1]
        dec = b_last - (br - ir)
        m_new = jnp.maximum(b_last + m_prev, jnp.max(dec, axis=-1, keepdims=True))
        upd = _dot((kt.astype(F32) * jnp.exp(dec - m_new)).astype(BF16), vext)
        c_ref[...] = jnp.exp(b_last + m_prev - m_new) * c_ref[...] + upd
        m_news.append(m_new)
    ws, scale = [], []
    for (_, _, _, mask, m_prev, _, _, _), (bc, br, ir), s_qk in zip(chains, gates, sc):
        a = jnp.where(mask, ir - br, -jnp.inf)
        mu = jnp.maximum(jnp.max(a, axis=-1, keepdims=True), m_prev)
        ws.append((jnp.exp(a - mu) * s_qk).astype(BF16))
        scale.append((jnp.exp(m_prev - mu), jnp.exp(-(bc + mu))))
    out = []
    for chain, w, q_c, (g_inter, floor), m_new in zip(chains, ws, qc, scale, m_news):
        vext, den_lane = chain[2], chain[7]
        nd = _dot(w, vext) + g_inter * q_c
        den = nd[:, den_lane:den_lane + 1]
        out.append((nd, jnp.maximum(jnp.abs(den), floor), m_new))
    return out


def _mlstm_body(xm_ref, om_ref, zm_ref, cw_ref, cb_ref, wq_ref, wk_ref, wkt_ref, wv_ref, gw_ref, gwt_ref,
                gb_ref, gbt_ref, skip_ref, ng_ref, tl_ref, tu_ref, o_ref,
                xc_s, q_s, kt_s, v_s, g_s, gr_s, nd_s, dd_s, c_s):
    s = xm_ref.shape[0]
    T = M_CHUNK
    n_chunks = s // T
    rb = min(M_ROW_BLOCK, s)
    halo = 16
    n_gates = 4 * M_HEADS

    lane_w = lax.broadcasted_iota(jnp.int32, (rb, M_WIN), 1)
    lane_g = lax.broadcasted_iota(jnp.int32, (rb, LANES), 1)
    gate_id = lax.broadcasted_iota(jnp.int32, (n_gates, rb), 0)
    for blk in range(s // rb):
        r0 = blk * rb
        mid = xm_ref[r0:r0 + rb, :].astype(F32)
        top = xm_ref[r0 - halo:r0, :].astype(F32) if blk > 0 else jnp.zeros((halo, M_WIDTH), F32)
        bot = (xm_ref[r0 + rb:r0 + rb + halo, :].astype(F32) if r0 + rb < s
               else jnp.zeros((halo, M_WIDTH), F32))
        xp = jnp.concatenate([top, mid, bot], axis=0)
        n_rows = rb + 2 * halo
        conv = jnp.zeros((rb, M_WIDTH), F32) + cb_ref[...]
        for j in range(M_CONV):
            shift = (M_CONV // 2 - j) % n_rows
            sh = xp if shift == 0 else pltpu.roll(xp, shift, 0)
            conv = conv + sh[halo:halo + rb] * cw_ref[j:j + 1, :]
        xc = _silu(conv)
        xc_s[r0:r0 + rb, :] = xc
        xcb = xc.astype(BF16)
        gates = jnp.zeros((rb, LANES), F32) + gb_ref[...]
        gates_t = jnp.zeros((n_gates, rb), F32) + gbt_ref[...]
        for h in range(M_HEADS):
            a = M_WIN_START[h]
            xw = xcb[:, a:a + M_WIN]
            qh = _dot(xw, wq_ref[h])
            kh = _dot(xw, wk_ref[h])
            kht = _nt(wkt_ref[h], xw)
            vh = _dot(xm_ref[r0:r0 + rb, a:a + M_WIN], wv_ref[h])
            qb, kb, vb = qh.astype(BF16), kh.astype(BF16), vh.astype(BF16)
            gates = gates + _dot(qb, gw_ref[0, h]) + _dot(kb, gw_ref[1, h]) + _dot(vb, gw_ref[2, h])
            gates_t = gates_t + _nt(gwt_ref[0, h], qb) + _nt(gwt_ref[1, h], kb) + _nt(gwt_ref[2, h], vb)
            q_s[h, r0:r0 + rb, :] = (qh * (M_HEAD_DIM ** -0.5)).astype(BF16)
            for a_c in range(rb // T):
                kt_s[h, r0 // T + a_c] = kht[:, a_c * T:(a_c + 1) * T].astype(BF16)
            v_s[h, r0:r0 + rb, :] = jnp.where(lane_w == M_DEN_LANE[h], 1.0, vh).astype(BF16)
        sp = _split_bf16(_log_sigmoid(gates))
        cum_f = _sum_halves(_dot(tl_ref[...], sp))
        cum_b = _sum_halves(_dot(tu_ref[...], sp))
        g_s[r0:r0 + rb, :] = jnp.where(lane_g >= 3 * M_HEADS, cum_b, cum_f)
        lf = _log_sigmoid(gates_t)
        hi = lf.astype(BF16)
        lo = (lf - hi.astype(F32)).astype(BF16)
        both = jnp.concatenate([hi, lo], axis=0)
        row_f = _dot(both, tu_ref[...])
        row_b = _dot(both, tl_ref[...])
        rows_t = jnp.where((gate_id >= M_HEADS) & (gate_id < 2 * M_HEADS), row_f[:n_gates] + row_f[n_gates:],
                           jnp.where(gate_id >= 3 * M_HEADS, row_b[:n_gates] + row_b[n_gates:], gates_t))
        for a_c in range(rb // T):
            gr_s[r0 // T + a_c] = rows_t[:, a_c * T:(a_c + 1) * T]

    c_s[...] = jnp.zeros_like(c_s)
    tt = lax.broadcasted_iota(jnp.int32, (T, T), 0)
    ss = lax.broadcasted_iota(jnp.int32, (T, T), 1)

    def step(n, ms):
        chains, where = [], []
        for d in range(2):
            fwd = d == 0
            c = n if fwd else n_chunks - 1 - n
            rows = pl.ds(pl.multiple_of(c * T, T), T)
            mask = (ss <= tt) if fwd else (ss >= tt)
            for h in range(M_HEADS):
                idx = d * M_HEADS + h
                chains.append((q_s[h, rows, :], kt_s[h, c], v_s[h, rows, :], mask, ms[idx], c_s.at[idx],
                               fwd, M_DEN_LANE[h]))
                where.append((h, rows, c))

        def gates_of():
            out = []
            for d in range(2):
                fwd = d == 0
                _, rows, c = where[d * M_HEADS]
                col = g_s[rows, :]
                row = gr_s[c]
                f0 = M_HEADS if fwd else 3 * M_HEADS
                i0 = 0 if fwd else 2 * M_HEADS
                for h in range(M_HEADS):
                    out.append((col[:, f0 + h:f0 + h + 1], row[f0 + h:f0 + h + 1, :],
                                row[i0 + h:i0 + h + 1, :]))
            return out

        new_ms = []
        for idx, ((h, rows, _), (nd, dd, m_new)) in enumerate(zip(where, _mlstm_chunks(chains, gates_of))):
            nd_s[idx // M_HEADS, h, rows, :] = nd.astype(BF16)
            dd_s[rows, idx:idx + 1] = dd
            new_ms.append(m_new)
        return tuple(new_ms)

    lax.fori_loop(0, n_chunks, step, tuple(jnp.zeros((1, 1), F32) for _ in range(2 * M_HEADS)))

    for blk in range(s // rb):
        r0 = blk * rb
        inv_dd = 1.0 / dd_s[r0:r0 + rb, 0:2 * M_HEADS]
        for h in range(M_HEADS):
            a = M_WIN_START[h]
            valid = (lane_w >= M_WIN_OFF[h]) & (lane_w < M_WIN_OFF[h] + M_HEAD_DIM)
            hsum = (nd_s[0, h, r0:r0 + rb, :].astype(F32) * inv_dd[:, h:h + 1]
                    + nd_s[1, h, r0:r0 + rb, :].astype(F32) * inv_dd[:, M_HEADS + h:M_HEADS + h + 1])
            hm = _gate_sigmoid(om_ref[r0:r0 + rb, a:a + M_WIN].astype(F32)) * jnp.where(valid, hsum, 0.0)
            ms = jnp.sum(hm * hm, axis=-1, keepdims=True) * (1.0 / M_HEAD_DIM)
            y = (hm * lax.rsqrt(ms + EPS)) * ng_ref[h] + skip_ref[h] * xc_s[r0:r0 + rb, a:a + M_WIN]
            y = y * _silu(zm_ref[r0:r0 + rb, a:a + M_WIN].astype(F32))
            o_ref[r0:r0 + rb, h * M_WIN:(h + 1) * M_WIN] = y.astype(o_ref.dtype)


def _mlstm(proj, params, layer):
    b, s, _ = proj.shape
    rb = min(M_ROW_BLOCK, s)
    t = np.arange(rb)
    tri = ((t[:, None] // M_CHUNK == t[None, :] // M_CHUNK) & (t[None, :] <= t[:, None])).astype(np.float32)
    seq = lambda c: pl.BlockSpec((None, s, M_WIDTH), lambda bi, c=c: (bi, 0, c))
    const = lambda shape: pl.BlockSpec(shape, lambda bi: (0,) * len(shape), pipeline_mode=pl.Buffered(1))
    of_layer = lambda a: pl.BlockSpec((None,) + a.shape[1:], lambda bi: (layer,) + (0,) * (a.ndim - 1),
                                      pipeline_mode=pl.Buffered(1))
    tris = (jnp.asarray(tri, BF16), jnp.asarray(tri.T, BF16))
    return pl.pallas_call(
        _mlstm_body,
        grid=(b,),
        in_specs=[seq(0), seq(1), seq(2)] + [of_layer(a) for a in params] + [const(t.shape) for t in tris],
        out_specs=pl.BlockSpec((None, s, M_HEADS * M_WIN), lambda bi: (bi, 0, 0), pipeline_mode=pl.Buffered(1)),
        out_shape=jax.ShapeDtypeStruct((b, s, M_HEADS * M_WIN), BF16),
        scratch_shapes=[
            pltpu.VMEM((s, M_WIDTH), F32),
            pltpu.VMEM((M_HEADS, s, M_WIN), BF16),
            pltpu.VMEM((M_HEADS, s // M_CHUNK, M_WIN, M_CHUNK), BF16),
            pltpu.VMEM((M_HEADS, s, M_WIN), BF16),
            pltpu.VMEM((s, LANES), F32),
            pltpu.VMEM((s // M_CHUNK, 4 * M_HEADS, M_CHUNK), F32),
            pltpu.VMEM((2, M_HEADS, s, M_WIN), BF16),
            pltpu.VMEM((s, LANES), F32),
            pltpu.VMEM((2 * M_HEADS, M_WIN, M_WIN), F32),
        ],
        compiler_params=pltpu.CompilerParams(
            dimension_semantics=("parallel",), vmem_limit_bytes=M_VMEM_LIMIT_BYTES),
        name="mlstm",
    )(proj, proj, proj, *params, *tris)


def _mlstm_params(m_conv_w, m_conv_b, m_wq, m_wk, m_wv, m_w_gates, m_b_gates, m_skip, m_norm_g, w_out_m):
    def dense(w):
        eye = jnp.eye(M_WIDTH // M_QKV_BLOCK, dtype=F32)
        return jnp.einsum("gh,gio->giho", eye, w).reshape(M_WIDTH, M_WIDTH)

    def head_pad(h, at_offset):
        lo = M_WIN_OFF[h] if at_offset else 0
        return (lo, M_WIN - M_HEAD_DIM - lo)

    def head_weights(w, at_offset):
        d = dense(w)
        out = []
        for h in range(M_HEADS):
            blk = d[M_WIN_START[h]:M_WIN_START[h] + M_WIN, h * M_HEAD_DIM:(h + 1) * M_HEAD_DIM]
            out.append(jnp.pad(blk, ((0, 0), head_pad(h, at_offset))))
        return jnp.stack(out).astype(BF16)

    def window(vec):
        out = []
        for h in range(M_HEADS):
            head = vec[h * M_HEAD_DIM:(h + 1) * M_HEAD_DIM]
            out.append(jnp.pad(head, (M_WIN_OFF[h], M_WIN - M_HEAD_DIM - M_WIN_OFF[h])))
        return jnp.stack(out)[:, None, :]

    n_gates = 4 * M_HEADS
    gw4 = m_w_gates.reshape(3, M_HEADS, M_HEAD_DIM, n_gates)
    gw = jnp.stack([
        jnp.stack([jnp.pad(gw4[x, h], (head_pad(h, x == 2), (0, 0))) for h in range(M_HEADS)])
        for x in range(3)])
    gwt = jnp.swapaxes(gw, -1, -2).astype(BF16)
    gw = jnp.pad(gw, ((0, 0), (0, 0), (0, 0), (0, LANES - n_gates))).astype(BF16)
    gb = jnp.pad(m_b_gates, (0, LANES - n_gates))[None, :]
    wo = []
    for h in range(M_HEADS):
        rows = w_out_m[h * M_HEAD_DIM:(h + 1) * M_HEAD_DIM]
        wo.append(jnp.pad(rows, (head_pad(h, True), (0, 0))))
    wk = head_weights(m_wk, False)
    return (m_conv_w, m_conv_b[None, :], head_weights(m_wq, False), wk, jnp.swapaxes(wk, 1, 2),
            head_weights(m_wv, True), gw, gwt, gb, m_b_gates[:, None], window(m_skip),
            window(m_norm_g)), jnp.concatenate(wo, axis=0).astype(BF16)


def kernel(x, positions, norm_g, w_in, m_conv_w, m_conv_b, m_wq, m_wk, m_wv, m_w_gates, m_b_gates,
           m_skip, m_norm_g, h_lb_logits, h_norm_g, a_lambda, a_norm_g, w_out, final_g):
    b, s, _ = x.shape
    half = ROPE_DIM // 2
    rope_inv = (ROPE_THETA ** (-jnp.arange(half, dtype=F32) / half))[:, None]
    pos = positions.astype(F32)[:, None, :]
    w_in_b = w_in.astype(BF16)
    w_out_b = w_out.astype(BF16)
    m_params, w_out_m = jax.vmap(_mlstm_params)(m_conv_w, m_conv_b, m_wq, m_wk, m_wv, m_w_gates, m_b_gates,
                                                m_skip, m_norm_g, w_out[:, :M_WIDTH])
    lb_logits = jnp.swapaxes(h_lb_logits, 0, 1)
    x2d = x.reshape(b * s, D_MODEL)
    for l in range(DEPTH):
        proj = _inproj(x2d, norm_g[:, None, :], w_in_b, l).reshape(b, s, IN_COLS)
        y_m = _mlstm(proj, m_params, l)
        y_h = _hgrn(proj, lb_logits, h_norm_g[:, None, :], l)
        lam_init = 0.8 - 0.6 * math.exp(-0.3 * l)
        y_a = _attention(proj, pos, rope_inv, a_lambda, a_norm_g[:, None, :], l, lam_init)
        x2d = _outproj(x2d, y_m.reshape(b * s, -1), y_h.reshape(b * s, -1), y_a.reshape(b * s, -1),
                       w_out_m, w_out_b, final_g[None, :], l, l == DEPTH - 1)
    return x2d.reshape(b, s, D_MODEL)
```

```python
import functools
import math

import numpy as np
import jax
import jax.numpy as jnp
from jax import lax
from jax.experimental import pallas as pl
from jax.experimental.pallas import tpu as pltpu

F32 = jnp.float32
BF16 = jnp.bfloat16

D_MODEL = 1024
DEPTH = 4
M_WIDTH = 768
M_HEADS = 4
M_HEAD_DIM = 192
M_QKV_BLOCK = 4
M_CONV = 5
H_WIDTH = 768
H_HEAD_DIM = 128
H_HEADS = 6
A_WIDTH = 512
A_HEADS = 4
A_VDIM = 128
A_QKDIM = 64
ROPE_DIM = 16
ROPE_THETA = 500000.0
IN_COLS = 3 * M_WIDTH + 5 * H_WIDTH + 4 * A_WIDTH
EPS = 1e-6

LANES = 128
MXU_DIM = 256
VMEM_LIMIT_BYTES = 56 * 1024 * 1024
M_VMEM_LIMIT_BYTES = 60 * 1024 * 1024

M_WIN = MXU_DIM
M_WIN_START = (0, 128, 384, 512)
M_WIN_OFF = (0, 64, 0, 64)
M_CHUNK = 128
M_ROW_BLOCK = 256
H_CHUNK = 128
H_LEVELS = 7
H_SUB = 64
H_SAFE_LOG_DECAY = -80.0
H_BLOCKS_PER_STEP = 4
M_DEN_LANE = (192, 0, 192, 0)
Q_BLOCK = 256


def _nt(a, b):
    return lax.dot_general(a, b, (((1,), (1,)), ((), ())), preferred_element_type=F32)


def _tn(a, b):
    return lax.dot_general(a, b, (((0,), (0,)), ((), ())), preferred_element_type=F32)


def _dot(a, b):
    return jnp.dot(a, b, preferred_element_type=F32)


def _sigmoid(x):
    return 1.0 / (1.0 + jnp.exp(-x))


def _gate_sigmoid(x):
    return 0.5 * jnp.tanh(0.5 * x) + 0.5


def _silu(x):
    return x * _gate_sigmoid(x)


def _split_bf16(x):
    hi = x.astype(BF16)
    lo = (x - hi.astype(F32)).astype(BF16)
    return jnp.concatenate([hi, lo], axis=1)


def _sum_halves(r):
    return r[:, :LANES] + r[:, LANES:]


def _inproj_body(x_ref, g_ref, w_ref, o_ref, h_ref):
    @pl.when(pl.program_id(1) == 0)
    def _():
        xf = x_ref[...]
        ms = jnp.mean(xf * xf, axis=-1, keepdims=True)
        h_ref[...] = ((xf * lax.rsqrt(ms + EPS)) * g_ref[...]).astype(BF16)

    o_ref[...] = _dot(h_ref[...], w_ref[...]).astype(o_ref.dtype)


def _inproj(x2d, g, w, layer):
    m = x2d.shape[0]
    bm = min(1024, m)
    bn = 4096
    return pl.pallas_call(
        _inproj_body,
        grid=(m // bm, IN_COLS // bn),
        in_specs=[
            pl.BlockSpec((bm, D_MODEL), lambda i, j: (i, 0)),
            pl.BlockSpec((None, 1, D_MODEL), lambda i, j: (layer, 0, 0)),
            pl.BlockSpec((None, D_MODEL, bn), lambda i, j: (layer, 0, j)),
        ],
        out_specs=pl.BlockSpec((bm, bn), lambda i, j: (i, j)),
        out_shape=jax.ShapeDtypeStruct((m, IN_COLS), BF16),
        scratch_shapes=[pltpu.VMEM((bm, D_MODEL), BF16)],
        compiler_params=pltpu.CompilerParams(
            dimension_semantics=("parallel", "arbitrary"), vmem_limit_bytes=VMEM_LIMIT_BYTES),
        name="inproj",
    )(x2d, g, w)


def _outproj_body(x_ref, ym_ref, yh_ref, ya_ref, wm_ref, wh_ref, wa_ref, fg_ref, o_ref, *, final):
    acc = x_ref[...] + _dot(ym_ref[...], wm_ref[...])
    acc = acc + _dot(yh_ref[...], wh_ref[...])
    acc = acc + _dot(ya_ref[...], wa_ref[...])
    if final:
        ms = jnp.mean(acc * acc, axis=-1, keepdims=True)
        acc = (acc * lax.rsqrt(ms + EPS)) * fg_ref[...]
    o_ref[...] = acc


def _outproj(x2d, ym, yh, ya, wm, w_out, fg, layer, final):
    m = x2d.shape[0]
    bm = min(1024, m)
    km, kh, ka = ym.shape[1], yh.shape[1], ya.shape[1]
    return pl.pallas_call(
        functools.partial(_outproj_body, final=final),
        grid=(m // bm,),
        in_specs=[
            pl.BlockSpec((bm, D_MODEL), lambda i: (i, 0)),
            pl.BlockSpec((bm, km), lambda i: (i, 0)),
            pl.BlockSpec((bm, kh), lambda i: (i, 0)),
            pl.BlockSpec((bm, ka), lambda i: (i, 0)),
            pl.BlockSpec((None, km, D_MODEL), lambda i: (layer, 0, 0)),
            pl.BlockSpec((None, kh, D_MODEL), lambda i: (layer, M_WIDTH // kh, 0)),
            pl.BlockSpec((None, ka, D_MODEL), lambda i: (layer, (M_WIDTH + H_WIDTH) // ka, 0)),
            pl.BlockSpec((1, D_MODEL), lambda i: (0, 0)),
        ],
        out_specs=pl.BlockSpec((bm, D_MODEL), lambda i: (i, 0)),
        out_shape=jax.ShapeDtypeStruct((m, D_MODEL), F32),
        input_output_aliases={0: 0} if layer > 0 else {},
        compiler_params=pltpu.CompilerParams(
            dimension_semantics=("parallel",), vmem_limit_bytes=VMEM_LIMIT_BYTES),
        name="outproj",
    )(x2d, ym, yh, ya, wm, w_out, w_out, fg)


def _rope(x, c, sa, sb):
    return x * c + pltpu.roll(x, LANES - ROPE_DIM // 2, 1) * sa + pltpu.roll(x, ROPE_DIM // 2, 1) * sb


def _attn_body(q_ref, k_ref, v_ref, z_ref, pos_ref, inv_ref, lam_ref, ng_ref, o_ref, kr_ref, vt_ref,
               c_ref, sa_ref, sb_ref, *, lam_init):
    i = pl.program_id(1)
    bq = q_ref.shape[0]
    s = k_ref.shape[0]

    @pl.when(i == 0)
    def _():
        ang = inv_ref[...] * pos_ref[...]
        cos, sin = jnp.cos(ang), jnp.sin(ang)
        half = ROPE_DIM // 2
        ones = jnp.ones((A_QKDIM - ROPE_DIM, s), F32)
        zeros = lambda n: jnp.zeros((n, s), F32)
        two = lambda parts: jnp.concatenate(parts + parts, axis=0)
        tables = (two([cos, cos, ones]), two([-sin, zeros(A_QKDIM - half)]),
                  two([zeros(half), sin, zeros(A_QKDIM - ROPE_DIM)]))
        for src, dst in zip(tables, (c_ref, sa_ref, sb_ref)):
            for blk in range(s // LANES):
                rows = slice(blk * LANES, (blk + 1) * LANES)
                dst[rows, :] = src[:, rows].T
        c, sa, sb = c_ref[...], sa_ref[...], sb_ref[...]
        for p in range(A_HEADS):
            sl = slice(p * LANES, (p + 1) * LANES)
            kr_ref[:, sl] = _rope(k_ref[:, sl].astype(F32), c, sa, sb).astype(BF16)
            for blk in range(s // bq):
                rows = slice(blk * bq, (blk + 1) * bq)
                vt_ref[p, :, rows] = v_ref[rows, sl].astype(F32).T.astype(BF16)

    lp = lam_ref[...]
    lam = (jnp.exp(jnp.sum(lp[0:1] * lp[1:2], axis=-1, keepdims=True))
           - jnp.exp(jnp.sum(lp[2:3] * lp[3:4], axis=-1, keepdims=True)) + lam_init)

    r0 = pl.multiple_of(i * bq, bq)
    c = c_ref[pl.ds(r0, bq), :]
    sa = sa_ref[pl.ds(r0, bq), :]
    sb = sb_ref[pl.ds(r0, bq), :]
    first = lax.broadcasted_iota(jnp.int32, (LANES, bq), 0) < A_QKDIM

    def scores(p):
        sl = slice(p * LANES, (p + 1) * LANES)
        qt = (_rope(q_ref[:, sl].astype(F32), c, sa, sb) * (A_QKDIM ** -0.5 * math.log2(math.e))).T
        ks = kr_ref[:, sl]
        return (_dot(ks, jnp.where(first, qt, 0.0).astype(BF16)),
                _dot(ks, jnp.where(first, 0.0, qt).astype(BF16)))

    nxt = scores(0)
    for p in range(A_HEADS):
        sl = slice(p * LANES, (p + 1) * LANES)
        s1, s2 = nxt
        if p + 1 < A_HEADS:
            nxt = scores(p + 1)
        e1 = jnp.exp2(s1 - jnp.max(s1, axis=0, keepdims=True))
        e2 = jnp.exp2(s2 - jnp.max(s2, axis=0, keepdims=True))
        n1 = 1.0 / jnp.sum(e1, axis=0, keepdims=True)
        n2 = lam / jnp.sum(e2, axis=0, keepdims=True)
        ot = _dot(vt_ref[p], e1.astype(BF16)) * n1 - _dot(vt_ref[p], e2.astype(BF16)) * n2
        o = ot.T
        ms = jnp.mean(o * o, axis=-1, keepdims=True)
        y = (o * lax.rsqrt(ms + EPS)) * ng_ref[:, sl] * (1.0 - lam_init)
        o_ref[:, sl] = (y * _silu(z_ref[:, sl].astype(F32))).astype(o_ref.dtype)


def _attention(proj, pos, inv, lam_p, ng, layer, lam_init):
    b, s, _ = proj.shape
    bq = min(Q_BLOCK, s)
    col0 = (3 * M_WIDTH + 5 * H_WIDTH) // A_WIDTH
    seq_full = lambda c: pl.BlockSpec((None, s, A_WIDTH), lambda bi, i, c=c: (bi, 0, c))
    seq_blk = lambda c: pl.BlockSpec((None, bq, A_WIDTH), lambda bi, i, c=c: (bi, i, c))
    return pl.pallas_call(
        functools.partial(_attn_body, lam_init=lam_init),
        grid=(b, s // bq),
        in_specs=[seq_blk(col0), seq_full(col0 + 1), seq_full(col0 + 2), seq_blk(col0 + 3),
                  pl.BlockSpec((None, 1, s), lambda bi, i: (bi, 0, 0)),
                  pl.BlockSpec((ROPE_DIM // 2, 1), lambda bi, i: (0, 0)),
                  pl.BlockSpec((None, 4, A_QKDIM), lambda bi, i: (layer, 0, 0)),
                  pl.BlockSpec((None, 1, A_WIDTH), lambda bi, i: (layer, 0, 0))],
        out_specs=pl.BlockSpec((None, bq, A_WIDTH), lambda bi, i: (bi, i, 0)),
        out_shape=jax.ShapeDtypeStruct((b, s, A_WIDTH), BF16),
        scratch_shapes=[pltpu.VMEM((s, A_WIDTH), BF16),
                        pltpu.VMEM((A_HEADS, A_VDIM, s), BF16),
                        pltpu.VMEM((s, LANES), F32), pltpu.VMEM((s, LANES), F32),
                        pltpu.VMEM((s, LANES), F32)],
        compiler_params=pltpu.CompilerParams(
            dimension_semantics=("parallel", "arbitrary"), vmem_limit_bytes=VMEM_LIMIT_BYTES),
        name="diff_attn",
    )(proj, proj, proj, proj, pos, inv, lam_p, ng)


def _hgrn_consts():
    t = np.arange(H_CHUNK)
    tt, r = t[:, None], t[None, :]
    blocks = []
    for lvl in range(H_LEVELS):
        start = (t >> lvl) << lvl
        end = start + (1 << lvl) - 1
        is_right = ((t >> lvl) & 1) == 1
        prefix = (r >= start[:, None]) & (r <= tt)
        suffix = (r > tt) & (r <= end[:, None])
        blocks.append(np.where(is_right[:, None], prefix, suffix))
    blocks.append(r <= tt)
    blocks.append(r > tt)
    w_f = np.stack(blocks).astype(np.float32)
    w_b = w_f[:, ::-1, ::-1]
    x = tt ^ r
    top = np.where(x > 0, np.floor(np.log2(np.maximum(x, 1))), -1).astype(np.int32)
    lvl_f = np.where(tt > r, top, -1).astype(np.int32)
    nblk = H_LEVELS + 2
    return (w_f.reshape(nblk * H_CHUNK, H_CHUNK), np.ascontiguousarray(w_b).reshape(nblk * H_CHUNK, H_CHUNK),
            lvl_f, np.ascontiguousarray(lvl_f.T))


def _hgrn_chunk(q, k, g, v, vt, w_ref, lvl, st_ref, fwd):
    T = H_CHUNK
    e = _sum_halves(_dot(w_ref[...], _split_bf16(g)))
    trow = lax.broadcasted_iota(jnp.int32, (T, 1), 0)
    scores = jnp.zeros((T, T), F32)
    for lv in range(H_LEVELS):
        q_role = ((trow >> lv) & 1) == (1 if fwd else 0)
        x = (jnp.where(q_role, q, k) * jnp.exp(e[lv * T:(lv + 1) * T])).astype(BF16)
        scores = jnp.where(lvl == lv, _nt(x, x), scores)
    diag = jnp.sum(q * k, axis=-1, keepdims=True)
    o = _dot(scores.astype(BF16), v) + diag * v.astype(F32)
    e_q = e[H_LEVELS * T:(H_LEVELS + 1) * T]
    e_k = e[(H_LEVELS + 1) * T:(H_LEVELS + 2) * T]
    st = st_ref[...]
    o = o + _nt((q * jnp.exp(e_q)).astype(BF16), st.astype(BF16))
    kt = (k * jnp.exp(e_k)).astype(BF16)
    e_tot = e_q[T - 1:T] if fwd else e_q[0:1]
    st_ref[...] = st * jnp.exp(e_tot) + _dot(vt, kt)
    return o


def _hgrn_fast_blocks(chains):
    T = H_CHUNK
    assert T == 2 * H_SUB
    in_a = lax.broadcasted_iota(jnp.int32, (T, 1), 0) < H_SUB
    cums = [_sum_halves(_dot(tri, _split_bf16(g))) for (_, _, g, _, _, tri, _, _) in chains]
    q_sub, q_blk, k_inv, k_sub, k_blk, decays = [], [], [], [], [], []
    for (q, k, _, _, _, _, _, fwd), cum in zip(chains, cums):
        e_a = jnp.exp(cum[H_SUB - 1:H_SUB] if fwd else cum[0:1])
        e_b = jnp.exp(cum[T - 1:T] if fwd else cum[H_SUB:H_SUB + 1])
        in_first = in_a if fwd else jnp.logical_not(in_a)
        e_first, e_second = (e_a, e_b) if fwd else (e_b, e_a)
        qs = q * jnp.exp(cum)
        ki = k * jnp.exp(-cum)
        ks = ki * jnp.where(in_first, e_first, e_second)
        q_sub.append(qs.astype(BF16))
        q_blk.append(jnp.where(in_first, qs, qs * e_first).astype(BF16))
        k_inv.append(ki.astype(BF16))
        k_sub.append(ks.astype(BF16))
        k_blk.append(jnp.where(in_first, ks * e_second, ks).astype(BF16))
        decays.append(e_first * e_second)
    same = [_nt(qs, ki) for qs, ki in zip(q_sub, k_inv)]
    cross = [_nt(qs, ks) for qs, ks in zip(q_sub, k_sub)]
    incs = [_dot(vt, kb) for (_, _, _, _, vt, _, _, _), kb in zip(chains, k_blk)]
    outs = [_dot(jnp.where(kind == 1, sm, jnp.where(kind == 2, cr, 0.0)).astype(BF16), v)
            for (_, _, _, v, _, _, kind, _), sm, cr in zip(chains, same, cross)]
    return outs, q_blk, incs, decays


def _hgrn_body(q_ref, ff_ref, fb_ref, i_ref, z_ref, lb_ref, ng_ref, wf_ref, wb_ref, lf_ref, lbm_ref,
               tf_ref, tb_ref, mf_ref, mb_ref, o_ref, k_s, g_s, acc_s, st_s, vt_s, qt_s, inc_s, dec_s,
               *, layer):
    s = q_ref.shape[0]
    n_chunks = s // H_CHUNK
    n_sub = s // H_SUB
    lowest = None
    for d, pre_ref in enumerate((ff_ref, fb_ref)):
        z = lb_ref[d]
        p = jnp.exp(z - jnp.max(z, axis=0, keepdims=True))
        p = p / jnp.sum(p, axis=0, keepdims=True)
        lb = jnp.sum(p[1:layer + 1], axis=0, keepdims=True) if layer > 0 else jnp.zeros((1, LANES), F32)
        f = lb + (1.0 - lb) * _sigmoid(pre_ref[...].astype(F32))
        g = jnp.log(f)
        g_s[d] = g
        k_s[d] = 1.0 - f
        tot = jnp.min(jnp.sum(g.reshape(n_sub, H_SUB, LANES), axis=1))
        lowest = tot if lowest is None else jnp.minimum(lowest, tot)
    for c in range(n_chunks):
        vt_s[c] = i_ref[c * H_CHUNK:(c + 1) * H_CHUNK, :].astype(F32).T.astype(BF16)

    def block_args(c, d):
        rows = pl.ds(pl.multiple_of(c * H_CHUNK, H_CHUNK), H_CHUNK)
        return rows, (q_ref[rows, :].astype(F32), k_s[d, rows, :], g_s[d, rows, :], i_ref[rows, :], vt_s[c])

    def fast():
        tris = (tf_ref[...], tb_ref[...])
        kinds = (mf_ref[...], mb_ref[...])
        per_step = math.gcd(H_BLOCKS_PER_STEP, n_chunks)

        def intra(i, carry):
            where, chains = [], []
            for u in range(per_step):
                c = i * per_step + u
                for d in range(2):
                    rows, args = block_args(c, d)
                    where.append((c, d, rows))
                    chains.append(args + (tris[d], kinds[d], d == 0))
            outs, q_blk, incs, decays = _hgrn_fast_blocks(chains)
            for (c, d, rows), o, qb, inc, dec in zip(where, outs, q_blk, incs, decays):
                if d == 0:
                    acc_s[rows, :] = o
                else:
                    acc_s[rows, :] += o
                qt_s[d, rows, :] = qb
                inc_s[d, c] = inc
                dec_s[d, c] = dec
            return carry

        lax.fori_loop(0, n_chunks // per_step, intra, 0)

        def scan(j, states):
            new = []
            for d in range(2):
                c = j if d == 0 else n_chunks - 1 - j
                rows = pl.ds(pl.multiple_of(c * H_CHUNK, H_CHUNK), H_CHUNK)
                acc_s[rows, :] += _nt(qt_s[d, rows, :], states[d].astype(BF16))
                new.append(states[d] * dec_s[d, c] + inc_s[d, c])
            return tuple(new)

        zero = jnp.zeros((H_HEAD_DIM, H_HEAD_DIM), F32)
        lax.fori_loop(0, n_chunks, scan, (zero, zero), unroll=4)

    def safe():
        acc_s[...] = jnp.zeros_like(acc_s)
        st_s[...] = jnp.zeros_like(st_s)
        lvl_f = lf_ref[...]
        lvl_b = lbm_ref[...]

        def step(n, carry):
            for d in range(2):
                rows, args = block_args(n if d == 0 else n_chunks - 1 - n, d)
                acc_s[rows, :] += _hgrn_chunk(*args, wf_ref if d == 0 else wb_ref,
                                              lvl_f if d == 0 else lvl_b, st_s.at[d], d == 0)
            return carry

        lax.fori_loop(0, n_chunks, step, 0)

    lax.cond(lowest >= H_SAFE_LOG_DECAY, fast, safe)
    o = acc_s[...]
    ms = jnp.mean(o * o, axis=-1, keepdims=True)
    y = (o * lax.rsqrt(ms + EPS)) * ng_ref[...]
    o_ref[...] = (y * _silu(z_ref[...].astype(F32))).astype(o_ref.dtype)


def _hgrn(proj, lb_logits, ng, layer):
    b, s, _ = proj.shape
    col0 = 3 * M_WIDTH // H_HEAD_DIM
    w_f, w_b, lvl_f, lvl_b = _hgrn_consts()
    tri_f, tri_b, kind_f, kind_b = _hgrn_fast_consts()
    seq = lambda g: pl.BlockSpec((None, s, H_HEAD_DIM), lambda bi, h, g=g: (bi, 0, col0 + g * H_HEADS + h))
    const = lambda shape: pl.BlockSpec(shape, lambda bi, h: (0,) * len(shape))
    consts = (jnp.asarray(w_f, BF16), jnp.asarray(w_b, BF16), jnp.asarray(lvl_f), jnp.asarray(lvl_b),
              jnp.asarray(tri_f, BF16), jnp.asarray(tri_b, BF16), jnp.asarray(kind_f), jnp.asarray(kind_b))
    n_chunks = s // H_CHUNK
    return pl.pallas_call(
        functools.partial(_hgrn_body, layer=layer),
        grid=(b, H_HEADS),
        in_specs=[seq(0), seq(1), seq(2), seq(3), seq(4),
                  pl.BlockSpec((2, DEPTH, H_HEAD_DIM), lambda bi, h: (0, 0, h)),
                  pl.BlockSpec((None, 1, H_HEAD_DIM), lambda bi, h: (layer, 0, h))]
                 + [const(c.shape) for c in consts],
        out_specs=pl.BlockSpec((None, s, H_HEAD_DIM), lambda bi, h: (bi, 0, h)),
        out_shape=jax.ShapeDtypeStruct((b, s, H_WIDTH), BF16),
        scratch_shapes=[pltpu.VMEM((2, s, H_HEAD_DIM), F32),
                        pltpu.VMEM((2, s, H_HEAD_DIM), F32),
                        pltpu.VMEM((s, H_HEAD_DIM), F32),
                        pltpu.VMEM((2, H_HEAD_DIM, H_HEAD_DIM), F32),
                        pltpu.VMEM((s // H_CHUNK, H_HEAD_DIM, H_CHUNK), BF16),
                        pltpu.VMEM((2, s, H_HEAD_DIM), BF16),
                        pltpu.VMEM((2, n_chunks, H_HEAD_DIM, H_HEAD_DIM), F32),
                        pltpu.VMEM((2, n_chunks, 1, H_HEAD_DIM), F32)],
        compiler_params=pltpu.CompilerParams(
            dimension_semantics=("parallel", "arbitrary"), vmem_limit_bytes=VMEM_LIMIT_BYTES),
        name="hgrn2",
    )(proj, proj, proj, proj, proj, lb_logits, ng, *consts)


def _hgrn_fast_consts():
    t = np.arange(H_CHUNK)
    tt, r = t[:, None], t[None, :]
    same = (tt // H_SUB) == (r // H_SUB)
    tri_f, tri_b = same & (r <= tt), same & (r >= tt)
    kind_f = np.where(tri_f, 1, np.where((tt >= H_SUB) & (r < H_SUB), 2, 0)).astype(np.int32)
    kind_b = np.where(tri_b, 1, np.where((tt < H_SUB) & (r >= H_SUB), 2, 0)).astype(np.int32)
    return tri_f.astype(np.float32), tri_b.astype(np.float32), kind_f, kind_b


def _log_sigmoid(x):
    return jnp.minimum(x, 0.0) - jnp.log(1.0 + jnp.exp(-jnp.abs(x)))


def _mlstm_chunks(chains, gates_of):
    T = M_CHUNK
    sc = [_dot(q, kt) for (q, kt, _, _, _, _, _, _) in chains]
    qc = [_dot(q, c_ref[...].astype(BF16)) for (q, _, _, _, _, c_ref, _, _) in chains]
    gates = gates_of()
    m_news = []
    for (_, kt, vext, _, m_prev, c_ref, fwd, _), (bc, br, ir) in zip(chains, gates):
        b_last = bc[T - 1:T] if fwd else bc[0:1]
        dec = b_last - (br - ir)
        m_new = jnp.maximum(b_last + m_prev, jnp.max(dec, axis=-1, keepdims=True))
        upd = _dot((kt.astype(F32) * jnp.exp(dec - m_new)).astype(BF16), vext)
        c_ref[...] = jnp.exp(b_last + m_prev - m_new) * c_ref[...] + upd
        m_news.append(m_new)
    ws, scale = [], []
    for (_, _, _, mask, m_prev, _, _, _), (bc, br, ir), s_qk in zip(chains, gates, sc):
        a = jnp.where(mask, ir - br, -jnp.inf)
        mu = jnp.maximum(jnp.max(a, axis=-1, keepdims=True), m_prev)
        ws.append((jnp.exp(a - mu) * s_qk).astype(BF16))
        scale.append((jnp.exp(m_prev - mu), jnp.exp(-(bc + mu))))
    out = []
    for chain, w, q_c, (g_inter, floor), m_new in zip(chains, ws, qc, scale, m_news):
        vext, den_lane = chain[2], chain[7]
        nd = _dot(w, vext) + g_inter * q_c
        den = nd[:, den_lane:den_lane + 1]
        out.append((nd, jnp.maximum(jnp.abs(den), floor), m_new))
    return out


def _mlstm_body(xm_ref, om_ref, zm_ref, cw_ref, cb_ref, wq_ref, wk_ref, wkt_ref, wv_ref, gw_ref, gwt_ref,
                gb_ref, gbt_ref, skip_ref, ng_ref, tl_ref, tu_ref, o_ref,
                xc_s, q_s, kt_s, v_s, g_s, gr_s, nd_s, dd_s, c_s):
    s = xm_ref.shape[0]
    T = M_CHUNK
    n_chunks = s // T
    rb = min(M_ROW_BLOCK, s)
    halo = 16
    n_gates = 4 * M_HEADS

    lane_w = lax.broadcasted_iota(jnp.int32, (rb, M_WIN), 1)
    lane_g = lax.broadcasted_iota(jnp.int32, (rb, LANES), 1)
    gate_id = lax.broadcasted_iota(jnp.int32, (n_gates, rb), 0)
    for blk in range(s // rb):
        r0 = blk * rb
        mid = xm_ref[r0:r0 + rb, :].astype(F32)
        top = xm_ref[r0 - halo:r0, :].astype(F32) if blk > 0 else jnp.zeros((halo, M_WIDTH), F32)
        bot = (xm_ref[r0 + rb:r0 + rb + halo, :].astype(F32) if r0 + rb < s
               else jnp.zeros((halo, M_WIDTH), F32))
        xp = jnp.concatenate([top, mid, bot], axis=0)
        n_rows = rb + 2 * halo
        conv = jnp.zeros((rb, M_WIDTH), F32) + cb_ref[...]
        for j in range(M_CONV):
            shift = (M_CONV // 2 - j) % n_rows
            sh = xp if shift == 0 else pltpu.roll(xp, shift, 0)
            conv = conv + sh[halo:halo + rb] * cw_ref[j:j + 1, :]
        xc = _silu(conv)
        xc_s[r0:r0 + rb, :] = xc
        xcb = xc.astype(BF16)
        gates = jnp.zeros((rb, LANES), F32) + gb_ref[...]
        gates_t = jnp.zeros((n_gates, rb), F32) + gbt_ref[...]
        for h in range(M_HEADS):
            a = M_WIN_START[h]
            xw = xcb[:, a:a + M_WIN]
            qh = _dot(xw, wq_ref[h])
            kh = _dot(xw, wk_ref[h])
            kht = _nt(wkt_ref[h], xw)
            vh = _dot(xm_ref[r0:r0 + rb, a:a + M_WIN], wv_ref[h])
            qb, kb, vb = qh.astype(BF16), kh.astype(BF16), vh.astype(BF16)
            gates = gates + _dot(qb, gw_ref[0, h]) + _dot(kb, gw_ref[1, h]) + _dot(vb, gw_ref[2, h])
            gates_t = gates_t + _nt(gwt_ref[0, h], qb) + _nt(gwt_ref[1, h], kb) + _nt(gwt_ref[2, h], vb)
            q_s[h, r0:r0 + rb, :] = (qh * (M_HEAD_DIM ** -0.5)).astype(BF16)
            for a_c in range(rb // T):
                kt_s[h, r0 // T + a_c] = kht[:, a_c * T:(a_c + 1) * T].astype(BF16)
            v_s[h, r0:r0 + rb, :] = jnp.where(lane_w == M_DEN_LANE[h], 1.0, vh).astype(BF16)
        sp = _split_bf16(_log_sigmoid(gates))
        cum_f = _sum_halves(_dot(tl_ref[...], sp))
        cum_b = _sum_halves(_dot(tu_ref[...], sp))
        g_s[r0:r0 + rb, :] = jnp.where(lane_g >= 3 * M_HEADS, cum_b, cum_f)
        lf = _log_sigmoid(gates_t)
        hi = lf.astype(BF16)
        lo = (lf - hi.astype(F32)).astype(BF16)
        both = jnp.concatenate([hi, lo], axis=0)
        row_f = _dot(both, tu_ref[...])
        row_b = _dot(both, tl_ref[...])
        rows_t = jnp.where((gate_id >= M_HEADS) & (gate_id < 2 * M_HEADS), row_f[:n_gates] + row_f[n_gates:],
                           jnp.where(gate_id >= 3 * M_HEADS, row_b[:n_gates] + row_b[n_gates:], gates_t))
        for a_c in range(rb // T):
            gr_s[r0 // T + a_c] = rows_t[:, a_c * T:(a_c + 1) * T]

    c_s[...] = jnp.zeros_like(c_s)
    tt = lax.broadcasted_iota(jnp.int32, (T, T), 0)
    ss = lax.broadcasted_iota(jnp.int32, (T, T), 1)

    def step(n, ms):
        chains, where = [], []
        for d in range(2):
            fwd = d == 0
            c = n if fwd else n_chunks - 1 - n
            rows = pl.ds(pl.multiple_of(c * T, T), T)
            mask = (ss <= tt) if fwd else (ss >= tt)
            for h in range(M_HEADS):
                idx = d * M_HEADS + h
                chains.append((q_s[h, rows, :], kt_s[h, c], v_s[h, rows, :], mask, ms[idx], c_s.at[idx],
                               fwd, M_DEN_LANE[h]))
                where.append((h, rows, c))

        def gates_of():
            out = []
            for d in range(2):
                fwd = d == 0
                _, rows, c = where[d * M_HEADS]
                col = g_s[rows, :]
                row = gr_s[c]
                f0 = M_HEADS if fwd else 3 * M_HEADS
                i0 = 0 if fwd else 2 * M_HEADS
                for h in range(M_HEADS):
                    out.append((col[:, f0 + h:f0 + h + 1], row[f0 + h:f0 + h + 1, :],
                                row[i0 + h:i0 + h + 1, :]))
            return out

        new_ms = []
        for idx, ((h, rows, _), (nd, dd, m_new)) in enumerate(zip(where, _mlstm_chunks(chains, gates_of))):
            nd_s[idx // M_HEADS, h, rows, :] = nd.astype(BF16)
            dd_s[rows, idx:idx + 1] = dd
            new_ms.append(m_new)
        return tuple(new_ms)

    lax.fori_loop(0, n_chunks, step, tuple(jnp.zeros((1, 1), F32) for _ in range(2 * M_HEADS)))

    for blk in range(s // rb):
        r0 = blk * rb
        inv_dd = 1.0 / dd_s[r0:r0 + rb, 0:2 * M_HEADS]
        for h in range(M_HEADS):
            a = M_WIN_START[h]
            valid = (lane_w >= M_WIN_OFF[h]) & (lane_w < M_WIN_OFF[h] + M_HEAD_DIM)
            hsum = (nd_s[0, h, r0:r0 + rb, :].astype(F32) * inv_dd[:, h:h + 1]
                    + nd_s[1, h, r0:r0 + rb, :].astype(F32) * inv_dd[:, M_HEADS + h:M_HEADS + h + 1])
            hm = _gate_sigmoid(om_ref[r0:r0 + rb, a:a + M_WIN].astype(F32)) * jnp.where(valid, hsum, 0.0)
            ms = jnp.sum(hm * hm, axis=-1, keepdims=True) * (1.0 / M_HEAD_DIM)
            y = (hm * lax.rsqrt(ms + EPS)) * ng_ref[h] + skip_ref[h] * xc_s[r0:r0 + rb, a:a + M_WIN]
            y = y * _silu(zm_ref[r0:r0 + rb, a:a + M_WIN].astype(F32))
            o_ref[r0:r0 + rb, h * M_WIN:(h + 1) * M_WIN] = y.astype(o_ref.dtype)


def _mlstm(proj, params, layer):
    b, s, _ = proj.shape
    rb = min(M_ROW_BLOCK, s)
    t = np.arange(rb)
    tri = ((t[:, None] // M_CHUNK == t[None, :] // M_CHUNK) & (t[None, :] <= t[:, None])).astype(np.float32)
    seq = lambda c: pl.BlockSpec((None, s, M_WIDTH), lambda bi, c=c: (bi, 0, c))
    const = lambda shape: pl.BlockSpec(shape, lambda bi: (0,) * len(shape), pipeline_mode=pl.Buffered(1))
    of_layer = lambda a: pl.BlockSpec((None,) + a.shape[1:], lambda bi: (layer,) + (0,) * (a.ndim - 1),
                                      pipeline_mode=pl.Buffered(1))
    tris = (jnp.asarray(tri, BF16), jnp.asarray(tri.T, BF16))
    return pl.pallas_call(
        _mlstm_body,
        grid=(b,),
        in_specs=[seq(0), seq(1), seq(2)] + [of_layer(a) for a in params] + [const(t.shape) for t in tris],
        out_specs=pl.BlockSpec((None, s, M_HEADS * M_WIN), lambda bi: (bi, 0, 0), pipeline_mode=pl.Buffered(1)),
        out_shape=jax.ShapeDtypeStruct((b, s, M_HEADS * M_WIN), BF16),
        scratch_shapes=[
            pltpu.VMEM((s, M_WIDTH), F32),
            pltpu.VMEM((M_HEADS, s, M_WIN), BF16),
            pltpu.VMEM((M_HEADS, s // M_CHUNK, M_WIN, M_CHUNK), BF16),
            pltpu.VMEM((M_HEADS, s, M_WIN), BF16),
            pltpu.VMEM((s, LANES), F32),
            pltpu.VMEM((s // M_CHUNK, 4 * M_HEADS, M_CHUNK), F32),
            pltpu.VMEM((2, M_HEADS, s, M_WIN), BF16),
            pltpu.VMEM((s, LANES), F32),
            pltpu.VMEM((2 * M_HEADS, M_WIN, M_WIN), F32),
        ],
        compiler_params=pltpu.CompilerParams(
            dimension_semantics=("parallel",), vmem_limit_bytes=M_VMEM_LIMIT_BYTES),
        name="mlstm",
    )(proj, proj, proj, *params, *tris)


def _mlstm_params(m_conv_w, m_conv_b, m_wq, m_wk, m_wv, m_w_gates, m_b_gates, m_skip, m_norm_g, w_out_m):
    def dense(w):
        n_blk = M_WIDTH // M_QKV_BLOCK
        col = np.arange(M_WIDTH)
        place = ((col[None, None, :] // M_QKV_BLOCK == np.arange(n_blk)[None, :, None])
                 & (col[None, None, :] % M_QKV_BLOCK == np.arange(M_QKV_BLOCK)[:, None, None]))
        return jnp.einsum("gio,ogc->gic", w, place.astype(np.float32),
                          precision=lax.Precision.HIGHEST).reshape(M_WIDTH, M_WIDTH)

    def head_pad(h, at_offset):
        lo = M_WIN_OFF[h] if at_offset else 0
        return (lo, M_WIN - M_HEAD_DIM - lo)

    def head_weights(w, at_offset):
        d = dense(w)
        out = []
        for h in range(M_HEADS):
            blk = d[M_WIN_START[h]:M_WIN_START[h] + M_WIN, h * M_HEAD_DIM:(h + 1) * M_HEAD_DIM]
            out.append(jnp.pad(blk, ((0, 0), head_pad(h, at_offset))))
        return jnp.stack(out).astype(BF16)

    def window(vec):
        out = []
        for h in range(M_HEADS):
            head = vec[h * M_HEAD_DIM:(h + 1) * M_HEAD_DIM]
            out.append(jnp.pad(head, (M_WIN_OFF[h], M_WIN - M_HEAD_DIM - M_WIN_OFF[h])))
        return jnp.stack(out)[:, None, :]

    n_gates = 4 * M_HEADS
    gw4 = m_w_gates.reshape(3, M_HEADS, M_HEAD_DIM, n_gates)
    gw = jnp.stack([
        jnp.stack([jnp.pad(gw4[x, h], (head_pad(h, x == 2), (0, 0))) for h in range(M_HEADS)])
        for x in range(3)])
    gwt = jnp.swapaxes(gw, -1, -2).astype(BF16)
    gw = jnp.pad(gw, ((0, 0), (0, 0), (0, 0), (0, LANES - n_gates))).astype(BF16)
    gb = jnp.pad(m_b_gates, (0, LANES - n_gates))[None, :]
    wo = []
    for h in range(M_HEADS):
        rows = w_out_m[h * M_HEAD_DIM:(h + 1) * M_HEAD_DIM]
        wo.append(jnp.pad(rows, (head_pad(h, True), (0, 0))))
    wk = head_weights(m_wk, False)
    return (m_conv_w, m_conv_b[None, :], head_weights(m_wq, False), wk, jnp.swapaxes(wk, 1, 2),
            head_weights(m_wv, True), gw, gwt, gb, m_b_gates[:, None], window(m_skip),
            window(m_norm_g)), jnp.concatenate(wo, axis=0).astype(BF16)


def kernel(x, positions, norm_g, w_in, m_conv_w, m_conv_b, m_wq, m_wk, m_wv, m_w_gates, m_b_gates,
           m_skip, m_norm_g, h_lb_logits, h_norm_g, a_lambda, a_norm_g, w_out, final_g):
    b, s, _ = x.shape
    half = ROPE_DIM // 2
    rope_inv = (ROPE_THETA ** (-jnp.arange(half, dtype=F32) / half))[:, None]
    pos = positions.astype(F32)[:, None, :]
    w_in_b = w_in.astype(BF16)
    w_out_b = w_out.astype(BF16)
    m_params, w_out_m = jax.vmap(_mlstm_params)(m_conv_w, m_conv_b, m_wq, m_wk, m_wv, m_w_gates, m_b_gates,
                                                m_skip, m_norm_g, w_out[:, :M_WIDTH])
    lb_logits = jnp.swapaxes(h_lb_logits, 0, 1)
    x2d = x.reshape(b * s, D_MODEL)
    for l in range(DEPTH):
        proj = _inproj(x2d, norm_g[:, None, :], w_in_b, l).reshape(b, s, IN_COLS)
        y_m = _mlstm(proj, m_params, l)
        y_h = _hgrn(proj, lb_logits, h_norm_g[:, None, :], l)
        lam_init = 0.8 - 0.6 * math.exp(-0.3 * l)
        y_a = _attention(proj, pos, rope_inv, a_lambda, a_norm_g[:, None, :], l, lam_init)
        x2d = _outproj(x2d, y_m.reshape(b * s, -1), y_h.reshape(b * s, -1), y_a.reshape(b * s, -1),
                       w_out_m, w_out_b, final_g[None, :], l, l == DEPTH - 1)
    return x2d.reshape(b, s, D_MODEL)
```

```python
import functools
import math

import numpy as np
import jax
import jax.numpy as jnp
from jax import lax
from jax.experimental import pallas as pl
from jax.experimental.pallas import tpu as pltpu

F32 = jnp.float32
BF16 = jnp.bfloat16

D_MODEL = 1024
DEPTH = 4
M_WIDTH = 768
M_HEADS = 4
M_HEAD_DIM = 192
M_QKV_BLOCK = 4
M_CONV = 5
H_WIDTH = 768
H_HEAD_DIM = 128
H_HEADS = 6
A_WIDTH = 512
A_HEADS = 4
A_VDIM = 128
A_QKDIM = 64
ROPE_DIM = 16
ROPE_THETA = 500000.0
IN_COLS = 3 * M_WIDTH + 5 * H_WIDTH + 4 * A_WIDTH
EPS = 1e-6

LANES = 128
MXU_DIM = 256
VMEM_LIMIT_BYTES = 56 * 1024 * 1024
M_VMEM_LIMIT_BYTES = 60 * 1024 * 1024

M_WIN = MXU_DIM
M_WIN_START = (0, 128, 384, 512)
M_WIN_OFF = (0, 64, 0, 64)
M_CHUNK = 128
M_ROW_BLOCK = 256
H_CHUNK = 128
H_LEVELS = 7
H_SUB = 64
H_SAFE_LOG_DECAY = -80.0
H_BLOCKS_PER_STEP = 4
M_DEN_LANE = (192, 0, 192, 0)
Q_BLOCK = 512


def _nt(a, b):
    return lax.dot_general(a, b, (((1,), (1,)), ((), ())), preferred_element_type=F32)


def _tn(a, b):
    return lax.dot_general(a, b, (((0,), (0,)), ((), ())), preferred_element_type=F32)


def _dot(a, b):
    return jnp.dot(a, b, preferred_element_type=F32)


def _sigmoid(x):
    return 1.0 / (1.0 + jnp.exp(-x))


def _gate_sigmoid(x):
    return 0.5 * jnp.tanh(0.5 * x) + 0.5


def _silu(x):
    return x * _gate_sigmoid(x)


def _split_bf16(x):
    hi = x.astype(BF16)
    lo = (x - hi.astype(F32)).astype(BF16)
    return jnp.concatenate([hi, lo], axis=1)


def _sum_halves(r):
    return r[:, :LANES] + r[:, LANES:]


def _inproj_body(x_ref, g_ref, w_ref, o_ref):
    xf = x_ref[...]
    ms = jnp.mean(xf * xf, axis=-1, keepdims=True)
    h = ((xf * lax.rsqrt(ms + EPS)) * g_ref[...]).astype(BF16)
    o_ref[...] = _dot(h, w_ref[...]).astype(o_ref.dtype)


def _inproj(x2d, g, w, layer):
    m = x2d.shape[0]
    bm = min(512, m)
    return pl.pallas_call(
        _inproj_body,
        grid=(m // bm,),
        in_specs=[
            pl.BlockSpec((bm, D_MODEL), lambda i: (i, 0)),
            pl.BlockSpec((None, 1, D_MODEL), lambda i: (layer, 0, 0)),
            pl.BlockSpec((None, D_MODEL, IN_COLS), lambda i: (layer, 0, 0), pipeline_mode=pl.Buffered(1)),
        ],
        out_specs=pl.BlockSpec((bm, IN_COLS), lambda i: (i, 0)),
        out_shape=jax.ShapeDtypeStruct((m, IN_COLS), BF16),
        compiler_params=pltpu.CompilerParams(
            dimension_semantics=("parallel",), vmem_limit_bytes=VMEM_LIMIT_BYTES),
        name="inproj",
    )(x2d, g, w)


def _outproj_body(x_ref, ym_ref, yh_ref, ya_ref, wm_ref, wh_ref, wa_ref, fg_ref, o_ref, *, final):
    acc = x_ref[...] + _dot(ym_ref[...], wm_ref[...])
    acc = acc + _dot(yh_ref[...], wh_ref[...])
    acc = acc + _dot(ya_ref[...], wa_ref[...])
    if final:
        ms = jnp.mean(acc * acc, axis=-1, keepdims=True)
        acc = (acc * lax.rsqrt(ms + EPS)) * fg_ref[...]
    o_ref[...] = acc


def _outproj(x2d, ym, yh, ya, w_out, fg, layer, final):
    m = x2d.shape[0]
    bm = min(1024, m)
    km, kh, ka = ym.shape[1], yh.shape[1], ya.shape[1]
    return pl.pallas_call(
        functools.partial(_outproj_body, final=final),
        grid=(m // bm,),
        in_specs=[
            pl.BlockSpec((bm, D_MODEL), lambda i: (i, 0)),
            pl.BlockSpec((bm, km), lambda i: (i, 0)),
            pl.BlockSpec((bm, kh), lambda i: (i, 0)),
            pl.BlockSpec((bm, ka), lambda i: (i, 0)),
            pl.BlockSpec((None, km, D_MODEL), lambda i: (layer, 0, 0)),
            pl.BlockSpec((None, kh, D_MODEL), lambda i: (layer, M_WIDTH // kh, 0)),
            pl.BlockSpec((None, ka, D_MODEL), lambda i: (layer, (M_WIDTH + H_WIDTH) // ka, 0)),
            pl.BlockSpec((1, D_MODEL), lambda i: (0, 0)),
        ],
        out_specs=pl.BlockSpec((bm, D_MODEL), lambda i: (i, 0)),
        out_shape=jax.ShapeDtypeStruct((m, D_MODEL), F32),
        input_output_aliases={0: 0} if layer > 0 else {},
        compiler_params=pltpu.CompilerParams(
            dimension_semantics=("parallel",), vmem_limit_bytes=VMEM_LIMIT_BYTES),
        name="outproj",
    )(x2d, ym, yh, ya, w_out, w_out, w_out, fg)


def _rope(x, c, sa, sb):
    return x * c + pltpu.roll(x, LANES - ROPE_DIM // 2, 1) * sa + pltpu.roll(x, ROPE_DIM // 2, 1) * sb


def _attn_body(q_ref, k_ref, v_ref, z_ref, pos_ref, inv_ref, lam_ref, ng_ref, o_ref, kr_ref, vt_ref,
               c_ref, sa_ref, sb_ref, *, lam_init):
    i = pl.program_id(1)
    bq = q_ref.shape[0]
    s = k_ref.shape[0]

    @pl.when(i == 0)
    def _():
        ang = inv_ref[...] * pos_ref[...]
        cos, sin = jnp.cos(ang), jnp.sin(ang)
        half = ROPE_DIM // 2
        ones = jnp.ones((A_QKDIM - ROPE_DIM, s), F32)
        zeros = lambda n: jnp.zeros((n, s), F32)
        two = lambda parts: jnp.concatenate(parts + parts, axis=0)
        tables = (two([cos, cos, ones]), two([-sin, zeros(A_QKDIM - half)]),
                  two([zeros(half), sin, zeros(A_QKDIM - ROPE_DIM)]))
        for src, dst in zip(tables, (c_ref, sa_ref, sb_ref)):
            for blk in range(s // LANES):
                rows = slice(blk * LANES, (blk + 1) * LANES)
                dst[rows, :] = src[:, rows].T
        c, sa, sb = c_ref[...], sa_ref[...], sb_ref[...]
        for p in range(A_HEADS):
            sl = slice(p * LANES, (p + 1) * LANES)
            kr_ref[:, sl] = _rope(k_ref[:, sl].astype(F32), c, sa, sb).astype(BF16)
            for blk in range(s // bq):
                rows = slice(blk * bq, (blk + 1) * bq)
                vt_ref[p, :, rows] = v_ref[rows, sl].astype(F32).T.astype(BF16)

    lp = lam_ref[...]
    lam = (jnp.exp(jnp.sum(lp[0:1] * lp[1:2], axis=-1, keepdims=True))
           - jnp.exp(jnp.sum(lp[2:3] * lp[3:4], axis=-1, keepdims=True)) + lam_init)

    r0 = pl.multiple_of(i * bq, bq)
    c = c_ref[pl.ds(r0, bq), :]
    sa = sa_ref[pl.ds(r0, bq), :]
    sb = sb_ref[pl.ds(r0, bq), :]
    first = lax.broadcasted_iota(jnp.int32, (LANES, bq), 0) < A_QKDIM

    def scores(p):
        sl = slice(p * LANES, (p + 1) * LANES)
        qt = (_rope(q_ref[:, sl].astype(F32), c, sa, sb) * (A_QKDIM ** -0.5 * math.log2(math.e))).T
        ks = kr_ref[:, sl]
        return (_dot(ks, jnp.where(first, qt, 0.0).astype(BF16)),
                _dot(ks, jnp.where(first, 0.0, qt).astype(BF16)))

    nxt = scores(0)
    for p in range(A_HEADS):
        sl = slice(p * LANES, (p + 1) * LANES)
        s1, s2 = nxt
        if p + 1 < A_HEADS:
            nxt = scores(p + 1)
        e1 = jnp.exp2(s1 - jnp.max(s1, axis=0, keepdims=True))
        e2 = jnp.exp2(s2 - jnp.max(s2, axis=0, keepdims=True))
        n1 = 1.0 / jnp.sum(e1, axis=0, keepdims=True)
        n2 = lam / jnp.sum(e2, axis=0, keepdims=True)
        ot = _dot(vt_ref[p], e1.astype(BF16)) * n1 - _dot(vt_ref[p], e2.astype(BF16)) * n2
        o = ot.T
        ms = jnp.mean(o * o, axis=-1, keepdims=True)
        y = (o * lax.rsqrt(ms + EPS)) * ng_ref[:, sl] * (1.0 - lam_init)
        o_ref[:, sl] = (y * _silu(z_ref[:, sl].astype(F32))).astype(o_ref.dtype)


def _attention(proj, pos, inv, lam_p, ng, layer, lam_init):
    b, s, _ = proj.shape
    bq = min(Q_BLOCK, s)
    col0 = (3 * M_WIDTH + 5 * H_WIDTH) // A_WIDTH
    seq_full = lambda c: pl.BlockSpec((None, s, A_WIDTH), lambda bi, i, c=c: (bi, 0, c))
    seq_blk = lambda c: pl.BlockSpec((None, bq, A_WIDTH), lambda bi, i, c=c: (bi, i, c))
    return pl.pallas_call(
        functools.partial(_attn_body, lam_init=lam_init),
        grid=(b, s // bq),
        in_specs=[seq_blk(col0), seq_full(col0 + 1), seq_full(col0 + 2), seq_blk(col0 + 3),
                  pl.BlockSpec((None, 1, s), lambda bi, i: (bi, 0, 0)),
                  pl.BlockSpec((ROPE_DIM // 2, 1), lambda bi, i: (0, 0)),
                  pl.BlockSpec((None, 4, A_QKDIM), lambda bi, i: (layer, 0, 0)),
                  pl.BlockSpec((None, 1, A_WIDTH), lambda bi, i: (layer, 0, 0))],
        out_specs=pl.BlockSpec((None, bq, A_WIDTH), lambda bi, i: (bi, i, 0)),
        out_shape=jax.ShapeDtypeStruct((b, s, A_WIDTH), BF16),
        scratch_shapes=[pltpu.VMEM((s, A_WIDTH), BF16),
                        pltpu.VMEM((A_HEADS, A_VDIM, s), BF16),
                        pltpu.VMEM((s, LANES), F32), pltpu.VMEM((s, LANES), F32),
                        pltpu.VMEM((s, LANES), F32)],
        compiler_params=pltpu.CompilerParams(
            dimension_semantics=("parallel", "arbitrary"), vmem_limit_bytes=VMEM_LIMIT_BYTES),
        name="diff_attn",
    )(proj, proj, proj, proj, pos, inv, lam_p, ng)


def _hgrn_consts():
    t = np.arange(H_CHUNK)
    tt, r = t[:, None], t[None, :]
    blocks = []
    for lvl in range(H_LEVELS):
        start = (t >> lvl) << lvl
        end = start + (1 << lvl) - 1
        is_right = ((t >> lvl) & 1) == 1
        prefix = (r >= start[:, None]) & (r <= tt)
        suffix = (r > tt) & (r <= end[:, None])
        blocks.append(np.where(is_right[:, None], prefix, suffix))
    blocks.append(r <= tt)
    blocks.append(r > tt)
    w_f = np.stack(blocks).astype(np.float32)
    w_b = w_f[:, ::-1, ::-1]
    x = tt ^ r
    top = np.where(x > 0, np.floor(np.log2(np.maximum(x, 1))), -1).astype(np.int32)
    lvl_f = np.where(tt > r, top, -1).astype(np.int32)
    nblk = H_LEVELS + 2
    return (w_f.reshape(nblk * H_CHUNK, H_CHUNK), np.ascontiguousarray(w_b).reshape(nblk * H_CHUNK, H_CHUNK),
            lvl_f, np.ascontiguousarray(lvl_f.T))


def _hgrn_chunk(q, k, g, v, vt, w_ref, lvl, st_ref, fwd):
    T = H_CHUNK
    e = _sum_halves(_dot(w_ref[...], _split_bf16(g)))
    trow = lax.broadcasted_iota(jnp.int32, (T, 1), 0)
    scores = jnp.zeros((T, T), F32)
    for lv in range(H_LEVELS):
        q_role = ((trow >> lv) & 1) == (1 if fwd else 0)
        x = (jnp.where(q_role, q, k) * jnp.exp(e[lv * T:(lv + 1) * T])).astype(BF16)
        scores = jnp.where(lvl == lv, _nt(x, x), scores)
    diag = jnp.sum(q * k, axis=-1, keepdims=True)
    o = _dot(scores.astype(BF16), v) + diag * v.astype(F32)
    e_q = e[H_LEVELS * T:(H_LEVELS + 1) * T]
    e_k = e[(H_LEVELS + 1) * T:(H_LEVELS + 2) * T]
    st = st_ref[...]
    o = o + _nt((q * jnp.exp(e_q)).astype(BF16), st.astype(BF16))
    kt = (k * jnp.exp(e_k)).astype(BF16)
    e_tot = e_q[T - 1:T] if fwd else e_q[0:1]
    st_ref[...] = st * jnp.exp(e_tot) + _dot(vt, kt)
    return o


def _hgrn_fast_blocks(chains):
    T = H_CHUNK
    assert T == 2 * H_SUB
    in_a = lax.broadcasted_iota(jnp.int32, (T, 1), 0) < H_SUB
    cums = [_sum_halves(_dot(tri, _split_bf16(g))) for (_, _, g, _, _, tri, _, _) in chains]
    q_sub, q_blk, k_inv, k_sub, k_blk, decays = [], [], [], [], [], []
    for (q, k, _, _, _, _, _, fwd), cum in zip(chains, cums):
        e_a = jnp.exp(cum[H_SUB - 1:H_SUB] if fwd else cum[0:1])
        e_b = jnp.exp(cum[T - 1:T] if fwd else cum[H_SUB:H_SUB + 1])
        in_first = in_a if fwd else jnp.logical_not(in_a)
        e_first, e_second = (e_a, e_b) if fwd else (e_b, e_a)
        qs = q * jnp.exp(cum)
        ki = k * jnp.exp(-cum)
        ks = ki * jnp.where(in_first, e_first, e_second)
        q_sub.append(qs.astype(BF16))
        q_blk.append(jnp.where(in_first, qs, qs * e_first).astype(BF16))
        k_inv.append(ki.astype(BF16))
        k_sub.append(ks.astype(BF16))
        k_blk.append(jnp.where(in_first, ks * e_second, ks).astype(BF16))
        decays.append(e_first * e_second)
    same = [_nt(qs, ki) for qs, ki in zip(q_sub, k_inv)]
    cross = [_nt(qs, ks) for qs, ks in zip(q_sub, k_sub)]
    incs = [_dot(vt, kb) for (_, _, _, _, vt, _, _, _), kb in zip(chains, k_blk)]
    outs = [_dot(jnp.where(kind == 1, sm, jnp.where(kind == 2, cr, 0.0)).astype(BF16), v)
            for (_, _, _, v, _, _, kind, _), sm, cr in zip(chains, same, cross)]
    return outs, q_blk, incs, decays


def _hgrn_body(q_ref, ff_ref, fb_ref, i_ref, z_ref, lb_ref, ng_ref, wf_ref, wb_ref, lf_ref, lbm_ref,
               tf_ref, tb_ref, mf_ref, mb_ref, o_ref, k_s, g_s, acc_s, st_s, vt_s, qt_s, inc_s, dec_s,
               *, layer):
    s = q_ref.shape[0]
    n_chunks = s // H_CHUNK
    n_sub = s // H_SUB
    lowest = None
    for d, pre_ref in enumerate((ff_ref, fb_ref)):
        z = lb_ref[d]
        p = jnp.exp(z - jnp.max(z, axis=0, keepdims=True))
        p = p / jnp.sum(p, axis=0, keepdims=True)
        lb = jnp.sum(p[1:layer + 1], axis=0, keepdims=True) if layer > 0 else jnp.zeros((1, LANES), F32)
        f = lb + (1.0 - lb) * _sigmoid(pre_ref[...].astype(F32))
        g = jnp.log(f)
        g_s[d] = g
        k_s[d] = 1.0 - f
        tot = jnp.min(jnp.sum(g.reshape(n_sub, H_SUB, LANES), axis=1))
        lowest = tot if lowest is None else jnp.minimum(lowest, tot)
    for c in range(n_chunks):
        vt_s[c] = i_ref[c * H_CHUNK:(c + 1) * H_CHUNK, :].astype(F32).T.astype(BF16)

    def block_args(c, d):
        rows = pl.ds(pl.multiple_of(c * H_CHUNK, H_CHUNK), H_CHUNK)
        return rows, (q_ref[rows, :].astype(F32), k_s[d, rows, :], g_s[d, rows, :], i_ref[rows, :], vt_s[c])

    def fast():
        tris = (tf_ref[...], tb_ref[...])
        kinds = (mf_ref[...], mb_ref[...])
        per_step = math.gcd(H_BLOCKS_PER_STEP, n_chunks)

        def intra(i, carry):
            where, chains = [], []
            for u in range(per_step):
                c = i * per_step + u
                for d in range(2):
                    rows, args = block_args(c, d)
                    where.append((c, d, rows))
                    chains.append(args + (tris[d], kinds[d], d == 0))
            outs, q_blk, incs, decays = _hgrn_fast_blocks(chains)
            for (c, d, rows), o, qb, inc, dec in zip(where, outs, q_blk, incs, decays):
                if d == 0:
                    acc_s[rows, :] = o
                else:
                    acc_s[rows, :] += o
                qt_s[d, rows, :] = qb
                inc_s[d, c] = inc
                dec_s[d, c] = dec
            return carry

        lax.fori_loop(0, n_chunks // per_step, intra, 0)

        def scan(j, states):
            new = []
            for d in range(2):
                c = j if d == 0 else n_chunks - 1 - j
                rows = pl.ds(pl.multiple_of(c * H_CHUNK, H_CHUNK), H_CHUNK)
                acc_s[rows, :] += _nt(qt_s[d, rows, :], states[d].astype(BF16))
                new.append(states[d] * dec_s[d, c] + inc_s[d, c])
            return tuple(new)

        zero = jnp.zeros((H_HEAD_DIM, H_HEAD_DIM), F32)
        lax.fori_loop(0, n_chunks, scan, (zero, zero), unroll=4)

    def safe():
        acc_s[...] = jnp.zeros_like(acc_s)
        st_s[...] = jnp.zeros_like(st_s)
        lvl_f = lf_ref[...]
        lvl_b = lbm_ref[...]

        def step(n, carry):
            for d in range(2):
                rows, args = block_args(n if d == 0 else n_chunks - 1 - n, d)
                acc_s[rows, :] += _hgrn_chunk(*args, wf_ref if d == 0 else wb_ref,
                                              lvl_f if d == 0 else lvl_b, st_s.at[d], d == 0)
            return carry

        lax.fori_loop(0, n_chunks, step, 0)

    lax.cond(lowest >= H_SAFE_LOG_DECAY, fast, safe)
    o = acc_s[...]
    ms = jnp.mean(o * o, axis=-1, keepdims=True)
    y = (o * lax.rsqrt(ms + EPS)) * ng_ref[...]
    o_ref[...] = (y * _silu(z_ref[...].astype(F32))).astype(o_ref.dtype)


def _hgrn(proj, lb_logits, ng, layer):
    b, s, _ = proj.shape
    col0 = 3 * M_WIDTH // H_HEAD_DIM
    w_f, w_b, lvl_f, lvl_b = _hgrn_consts()
    tri_f, tri_b, kind_f, kind_b = _hgrn_fast_consts()
    seq = lambda g: pl.BlockSpec((None, s, H_HEAD_DIM), lambda bi, h, g=g: (bi, 0, col0 + g * H_HEADS + h))
    const = lambda shape: pl.BlockSpec(shape, lambda bi, h: (0,) * len(shape))
    consts = (jnp.asarray(w_f, BF16), jnp.asarray(w_b, BF16), jnp.asarray(lvl_f), jnp.asarray(lvl_b),
              jnp.asarray(tri_f, BF16), jnp.asarray(tri_b, BF16), jnp.asarray(kind_f), jnp.asarray(kind_b))
    n_chunks = s // H_CHUNK
    return pl.pallas_call(
        functools.partial(_hgrn_body, layer=layer),
        grid=(b, H_HEADS),
        in_specs=[seq(0), seq(1), seq(2), seq(3), seq(4),
                  pl.BlockSpec((2, DEPTH, H_HEAD_DIM), lambda bi, h: (0, 0, h)),
                  pl.BlockSpec((None, 1, H_HEAD_DIM), lambda bi, h: (layer, 0, h))]
                 + [const(c.shape) for c in consts],
        out_specs=pl.BlockSpec((None, s, H_HEAD_DIM), lambda bi, h: (bi, 0, h)),
        out_shape=jax.ShapeDtypeStruct((b, s, H_WIDTH), BF16),
        scratch_shapes=[pltpu.VMEM((2, s, H_HEAD_DIM), F32),
                        pltpu.VMEM((2, s, H_HEAD_DIM), F32),
                        pltpu.VMEM((s, H_HEAD_DIM), F32),
                        pltpu.VMEM((2, H_HEAD_DIM, H_HEAD_DIM), F32),
                        pltpu.VMEM((s // H_CHUNK, H_HEAD_DIM, H_CHUNK), BF16),
                        pltpu.VMEM((2, s, H_HEAD_DIM), BF16),
                        pltpu.VMEM((2, n_chunks, H_HEAD_DIM, H_HEAD_DIM), F32),
                        pltpu.VMEM((2, n_chunks, 1, H_HEAD_DIM), F32)],
        compiler_params=pltpu.CompilerParams(
            dimension_semantics=("parallel", "arbitrary"), vmem_limit_bytes=VMEM_LIMIT_BYTES),
        name="hgrn2",
    )(proj, proj, proj, proj, proj, lb_logits, ng, *consts)


def _hgrn_fast_consts():
    t = np.arange(H_CHUNK)
    tt, r = t[:, None], t[None, :]
    same = (tt // H_SUB) == (r // H_SUB)
    tri_f, tri_b = same & (r <= tt), same & (r >= tt)
    kind_f = np.where(tri_f, 1, np.where((tt >= H_SUB) & (r < H_SUB), 2, 0)).astype(np.int32)
    kind_b = np.where(tri_b, 1, np.where((tt < H_SUB) & (r >= H_SUB), 2, 0)).astype(np.int32)
    return tri_f.astype(np.float32), tri_b.astype(np.float32), kind_f, kind_b


def _log_sigmoid(x):
    return jnp.minimum(x, 0.0) - jnp.log(1.0 + jnp.exp(-jnp.abs(x)))


def _mlstm_chunks(chains, gates_of):
    T = M_CHUNK
    sc = [_dot(q, kt) for (q, kt, _, _, _, _, _, _) in chains]
    qc = [_dot(q, c_ref[...].astype(BF16)) for (q, _, _, _, _, c_ref, _, _) in chains]
    gates = gates_of()
    m_news = []
    for (_, kt, vext, _, m_prev, c_ref, fwd, _), (bc, br, ir) in zip(chains, gates):
        b_last = bc[T - 1:T] if fwd else bc[0:1]
        dec = b_last - (br - ir)
        m_new = jnp.maximum(b_last + m_prev, jnp.max(dec, axis=-1, keepdims=True))
        upd = _dot((kt.astype(F32) * jnp.exp(dec - m_new)).astype(BF16), vext)
        c_ref[...] = jnp.exp(b_last + m_prev - m_new) * c_ref[...] + upd
        m_news.append(m_new)
    ws, scale = [], []
    for (_, _, _, mask, m_prev, _, _, _), (bc, br, ir), s_qk in zip(chains, gates, sc):
        a = jnp.where(mask, ir - br, -jnp.inf)
        mu = jnp.maximum(jnp.max(a, axis=-1, keepdims=True), m_prev)
        ws.append((jnp.exp(a - mu) * s_qk).astype(BF16))
        scale.append((jnp.exp(m_prev - mu), jnp.exp(-(bc + mu))))
    out = []
    for chain, w, q_c, (g_inter, floor), m_new in zip(chains, ws, qc, scale, m_news):
        vext, den_lane = chain[2], chain[7]
        nd = _dot(w, vext) + g_inter * q_c
        den = nd[:, den_lane:den_lane + 1]
        out.append((nd, jnp.maximum(jnp.abs(den), floor), m_new))
    return out


def _mlstm_body(xm_ref, om_ref, zm_ref, cw_ref, cb_ref, wq_ref, wk_ref, wkt_ref, wv_ref, gw_ref, gwt_ref,
                gb_ref, gbt_ref, skip_ref, ng_ref, tl_ref, tu_ref, o_ref,
                xc_s, q_s, kt_s, v_s, g_s, gr_s, nd_s, dd_s, c_s):
    s = xm_ref.shape[0]
    T = M_CHUNK
    n_chunks = s // T
    rb = min(M_ROW_BLOCK, s)
    halo = 16
    n_gates = 4 * M_HEADS

    lane_w = lax.broadcasted_iota(jnp.int32, (rb, M_WIN), 1)
    lane_g = lax.broadcasted_iota(jnp.int32, (rb, LANES), 1)
    gate_id = lax.broadcasted_iota(jnp.int32, (n_gates, rb), 0)
    for blk in range(s // rb):
        r0 = blk * rb
        mid = xm_ref[r0:r0 + rb, :].astype(F32)
        top = xm_ref[r0 - halo:r0, :].astype(F32) if blk > 0 else jnp.zeros((halo, M_WIDTH), F32)
        bot = (xm_ref[r0 + rb:r0 + rb + halo, :].astype(F32) if r0 + rb < s
               else jnp.zeros((halo, M_WIDTH), F32))
        xp = jnp.concatenate([top, mid, bot], axis=0)
        n_rows = rb + 2 * halo
        conv = jnp.zeros((rb, M_WIDTH), F32) + cb_ref[...]
        for j in range(M_CONV):
            shift = (M_CONV // 2 - j) % n_rows
            sh = xp if shift == 0 else pltpu.roll(xp, shift, 0)
            conv = conv + sh[halo:halo + rb] * cw_ref[j:j + 1, :]
        xc = _silu(conv)
        xc_s[r0:r0 + rb, :] = xc
        xcb = xc.astype(BF16)
        gates = jnp.zeros((rb, LANES), F32) + gb_ref[...]
        gates_t = jnp.zeros((n_gates, rb), F32) + gbt_ref[...]
        for h in range(M_HEADS):
            a = M_WIN_START[h]
            xw = xcb[:, a:a + M_WIN]
            qh = _dot(xw, wq_ref[h])
            kh = _dot(xw, wk_ref[h])
            kht = _nt(wkt_ref[h], xw)
            vh = _dot(xm_ref[r0:r0 + rb, a:a + M_WIN], wv_ref[h])
            qb, kb, vb = qh.astype(BF16), kh.astype(BF16), vh.astype(BF16)
            gates = gates + _dot(qb, gw_ref[0, h]) + _dot(kb, gw_ref[1, h]) + _dot(vb, gw_ref[2, h])
            gates_t = gates_t + _nt(gwt_ref[0, h], qb) + _nt(gwt_ref[1, h], kb) + _nt(gwt_ref[2, h], vb)
            q_s[h, r0:r0 + rb, :] = (qh * (M_HEAD_DIM ** -0.5)).astype(BF16)
            for a_c in range(rb // T):
                kt_s[h, r0 // T + a_c] = kht[:, a_c * T:(a_c + 1) * T].astype(BF16)
            v_s[h, r0:r0 + rb, :] = jnp.where(lane_w == M_DEN_LANE[h], 1.0, vh).astype(BF16)
        sp = _split_bf16(_log_sigmoid(gates))
        cum_f = _sum_halves(_dot(tl_ref[...], sp))
        cum_b = _sum_halves(_dot(tu_ref[...], sp))
        g_s[r0:r0 + rb, :] = jnp.where(lane_g >= 3 * M_HEADS, cum_b, cum_f)
        lf = _log_sigmoid(gates_t)
        hi = lf.astype(BF16)
        lo = (lf - hi.astype(F32)).astype(BF16)
        both = jnp.concatenate([hi, lo], axis=0)
        row_f = _dot(both, tu_ref[...])
        row_b = _dot(both, tl_ref[...])
        rows_t = jnp.where((gate_id >= M_HEADS) & (gate_id < 2 * M_HEADS), row_f[:n_gates] + row_f[n_gates:],
                           jnp.where(gate_id >= 3 * M_HEADS, row_b[:n_gates] + row_b[n_gates:], gates_t))
        for a_c in range(rb // T):
            gr_s[r0 // T + a_c] = rows_t[:, a_c * T:(a_c + 1) * T]

    c_s[...] = jnp.zeros_like(c_s)
    tt = lax.broadcasted_iota(jnp.int32, (T, T), 0)
    ss = lax.broadcasted_iota(jnp.int32, (T, T), 1)

    def step(n, ms):
        chains, where = [], []
        for d in range(2):
            fwd = d == 0
            c = n if fwd else n_chunks - 1 - n
            rows = pl.ds(pl.multiple_of(c * T, T), T)
            mask = (ss <= tt) if fwd else (ss >= tt)
            for h in range(M_HEADS):
                idx = d * M_HEADS + h
                chains.append((q_s[h, rows, :], kt_s[h, c], v_s[h, rows, :], mask, ms[idx], c_s.at[idx],
                               fwd, M_DEN_LANE[h]))
                where.append((h, rows, c))

        def gates_of():
            out = []
            for d in range(2):
                fwd = d == 0
                _, rows, c = where[d * M_HEADS]
                col = g_s[rows, :]
                row = gr_s[c]
                f0 = M_HEADS if fwd else 3 * M_HEADS
                i0 = 0 if fwd else 2 * M_HEADS
                for h in range(M_HEADS):
                    out.append((col[:, f0 + h:f0 + h + 1], row[f0 + h:f0 + h + 1, :],
                                row[i0 + h:i0 + h + 1, :]))
            return out

        new_ms = []
        for idx, ((h, rows, _), (nd, dd, m_new)) in enumerate(zip(where, _mlstm_chunks(chains, gates_of))):
            nd_s[idx // M_HEADS, h, rows, :] = nd.astype(BF16)
            dd_s[rows, idx:idx + 1] = dd
            new_ms.append(m_new)
        return tuple(new_ms)

    lax.fori_loop(0, n_chunks, step, tuple(jnp.zeros((1, 1), F32) for _ in range(2 * M_HEADS)))

    lane_t = lax.broadcasted_iota(jnp.int32, (rb, LANES), 1)
    low = lane_t < M_HEAD_DIM - LANES
    for blk in range(s // rb):
        r0 = blk * rb
        inv_dd = 1.0 / dd_s[r0:r0 + rb, 0:2 * M_HEADS]
        hs = []
        for h in range(M_HEADS):
            valid = (lane_w >= M_WIN_OFF[h]) & (lane_w < M_WIN_OFF[h] + M_HEAD_DIM)
            hsum = (nd_s[0, h, r0:r0 + rb, :].astype(F32) * inv_dd[:, h:h + 1]
                    + nd_s[1, h, r0:r0 + rb, :].astype(F32) * inv_dd[:, M_HEADS + h:M_HEADS + h + 1])
            hs.append(jnp.where(valid, hsum, 0.0))
        tiles = []
        for p in range(M_HEADS // 2):
            h0, h1 = hs[2 * p], hs[2 * p + 1]
            tiles += [h0[:, :LANES], h0[:, LANES:] + h1[:, :LANES], h1[:, LANES:]]
        hm = _gate_sigmoid(om_ref[r0:r0 + rb, :].astype(F32)) * jnp.concatenate(tiles, axis=1)
        sq = hm * hm
        scales = []
        for p in range(M_HEADS // 2):
            t0, t1, t2 = (sq[:, (3 * p + j) * LANES:(3 * p + j + 1) * LANES] for j in range(3))
            ms0 = (jnp.sum(t0, axis=-1, keepdims=True)
                   + jnp.sum(jnp.where(low, t1, 0.0), axis=-1, keepdims=True)) * (1.0 / M_HEAD_DIM)
            ms1 = (jnp.sum(jnp.where(low, 0.0, t1), axis=-1, keepdims=True)
                   + jnp.sum(t2, axis=-1, keepdims=True)) * (1.0 / M_HEAD_DIM)
            r_0 = jnp.broadcast_to(lax.rsqrt(ms0 + EPS), (rb, LANES))
            r_1 = jnp.broadcast_to(lax.rsqrt(ms1 + EPS), (rb, LANES))
            scales += [r_0, jnp.where(low, r_0, r_1), r_1]
        y = (hm * jnp.concatenate(scales, axis=1)) * ng_ref[...] + skip_ref[...] * xc_s[r0:r0 + rb, :]
        y = y * _silu(zm_ref[r0:r0 + rb, :].astype(F32))
        o_ref[r0:r0 + rb, :] = y.astype(o_ref.dtype)


def _mlstm(proj, params, layer):
    b, s, _ = proj.shape
    rb = min(M_ROW_BLOCK, s)
    t = np.arange(rb)
    tri = ((t[:, None] // M_CHUNK == t[None, :] // M_CHUNK) & (t[None, :] <= t[:, None])).astype(np.float32)
    seq = lambda c: pl.BlockSpec((None, s, M_WIDTH), lambda bi, c=c: (bi, 0, c))
    const = lambda shape: pl.BlockSpec(shape, lambda bi: (0,) * len(shape), pipeline_mode=pl.Buffered(1))
    of_layer = lambda a: pl.BlockSpec((None,) + a.shape[1:], lambda bi: (layer,) + (0,) * (a.ndim - 1),
                                      pipeline_mode=pl.Buffered(1))
    tris = (jnp.asarray(tri, BF16), jnp.asarray(tri.T, BF16))
    return pl.pallas_call(
        _mlstm_body,
        grid=(b,),
        in_specs=[seq(0), seq(1), seq(2)] + [of_layer(a) for a in params] + [const(t.shape) for t in tris],
        out_specs=pl.BlockSpec((None, s, M_WIDTH), lambda bi: (bi, 0, 0), pipeline_mode=pl.Buffered(1)),
        out_shape=jax.ShapeDtypeStruct((b, s, M_WIDTH), BF16),
        scratch_shapes=[
            pltpu.VMEM((s, M_WIDTH), F32),
            pltpu.VMEM((M_HEADS, s, M_WIN), BF16),
            pltpu.VMEM((M_HEADS, s // M_CHUNK, M_WIN, M_CHUNK), BF16),
            pltpu.VMEM((M_HEADS, s, M_WIN), BF16),
            pltpu.VMEM((s, LANES), F32),
            pltpu.VMEM((s // M_CHUNK, 4 * M_HEADS, M_CHUNK), F32),
            pltpu.VMEM((2, M_HEADS, s, M_WIN), BF16),
            pltpu.VMEM((s, LANES), F32),
            pltpu.VMEM((2 * M_HEADS, M_WIN, M_WIN), F32),
        ],
        compiler_params=pltpu.CompilerParams(
            dimension_semantics=("parallel",), vmem_limit_bytes=M_VMEM_LIMIT_BYTES),
        name="mlstm",
    )(proj, proj, proj, *params, *tris)


def _mlstm_params(m_conv_w, m_conv_b, m_wq, m_wk, m_wv, m_w_gates, m_b_gates, m_skip, m_norm_g):
    def dense(w):
        n_blk = M_WIDTH // M_QKV_BLOCK
        col = np.arange(M_WIDTH)
        place = ((col[None, None, :] // M_QKV_BLOCK == np.arange(n_blk)[None, :, None])
                 & (col[None, None, :] % M_QKV_BLOCK == np.arange(M_QKV_BLOCK)[:, None, None]))
        return jnp.einsum("gio,ogc->gic", w, place.astype(np.float32),
                          precision=lax.Precision.HIGHEST).reshape(M_WIDTH, M_WIDTH)

    def head_pad(h, at_offset):
        lo = M_WIN_OFF[h] if at_offset else 0
        return (lo, M_WIN - M_HEAD_DIM - lo)

    def head_weights(w, at_offset):
        d = dense(w)
        out = []
        for h in range(M_HEADS):
            blk = d[M_WIN_START[h]:M_WIN_START[h] + M_WIN, h * M_HEAD_DIM:(h + 1) * M_HEAD_DIM]
            out.append(jnp.pad(blk, ((0, 0), head_pad(h, at_offset))))
        return jnp.stack(out).astype(BF16)

    n_gates = 4 * M_HEADS
    gw4 = m_w_gates.reshape(3, M_HEADS, M_HEAD_DIM, n_gates)
    gw = jnp.stack([
        jnp.stack([jnp.pad(gw4[x, h], (head_pad(h, x == 2), (0, 0))) for h in range(M_HEADS)])
        for x in range(3)])
    gwt = jnp.swapaxes(gw, -1, -2).astype(BF16)
    gw = jnp.pad(gw, ((0, 0), (0, 0), (0, 0), (0, LANES - n_gates))).astype(BF16)
    gb = jnp.pad(m_b_gates, (0, LANES - n_gates))[None, :]
    wk = head_weights(m_wk, False)
    return (m_conv_w, m_conv_b[None, :], head_weights(m_wq, False), wk, jnp.swapaxes(wk, 1, 2),
            head_weights(m_wv, True), gw, gwt, gb, m_b_gates[:, None], m_skip[None, :], m_norm_g[None, :])


def kernel(x, positions, norm_g, w_in, m_conv_w, m_conv_b, m_wq, m_wk, m_wv, m_w_gates, m_b_gates,
           m_skip, m_norm_g, h_lb_logits, h_norm_g, a_lambda, a_norm_g, w_out, final_g):
    b, s, _ = x.shape
    half = ROPE_DIM // 2
    rope_inv = (ROPE_THETA ** (-jnp.arange(half, dtype=F32) / half))[:, None]
    pos = positions.astype(F32)[:, None, :]
    w_in_b = w_in.astype(BF16)
    w_out_b = w_out.astype(BF16)
    m_params = jax.vmap(_mlstm_params)(m_conv_w, m_conv_b, m_wq, m_wk, m_wv, m_w_gates, m_b_gates,
                                       m_skip, m_norm_g)
    lb_logits = jnp.swapaxes(h_lb_logits, 0, 1)
    x2d = x.reshape(b * s, D_MODEL)
    for l in range(DEPTH):
        proj = _inproj(x2d, norm_g[:, None, :], w_in_b, l).reshape(b, s, IN_COLS)
        y_m = _mlstm(proj, m_params, l)
        y_h = _hgrn(proj, lb_logits, h_norm_g[:, None, :], l)
        lam_init = 0.8 - 0.6 * math.exp(-0.3 * l)
        y_a = _attention(proj, pos, rope_inv, a_lambda, a_norm_g[:, None, :], l, lam_init)
        x2d = _outproj(x2d, y_m.reshape(b * s, -1), y_h.reshape(b * s, -1), y_a.reshape(b * s, -1),
                       w_out_b, final_g[None, :], l, l == DEPTH - 1)
    return x2d.reshape(b, s, D_MODEL)
```
